```python
import jax
import jax.numpy as jnp
from jax import lax
import numpy as np

D_MODEL = 2048
BATCH = 4
SEQ = 8192
DEPTH = 2
DEC_BATCH = 32
DEC_SEQ = 32
PAST_LEN = 1024

CHUNK = 64
N_MIXERS = 2
N_CONV_LAYERS = (DEPTH + 1) // 2
N_RWKV_LAYERS = DEPTH // 2
CONV_WIDTH = 3
HEAD_SIZE = 64
N_HEADS = D_MODEL // HEAD_SIZE
D_DECAY_LORA = 96
D_AAA_LORA = 96
D_GATE_LORA = 256
N_SHIFT_MIX = 6
N_GROUPS = 4
EXPERTS_PER_GROUP = 4
N_EXPERTS = N_GROUPS * EXPERTS_PER_GROUP
TOP_K_IN_GROUP = 2
D_EXPERT = D_MODEL // 4
N_ADA = 6
RMS_EPS = 1e-6
GN_EPS = 64e-5

kernel_name = 'hybrid_shortconv_rwkv7_hmoe_stream_step'


def rms_norm(x, gain):
    xf = x.astype(jnp.float32)
    y = xf * lax.rsqrt(jnp.mean(xf * xf, axis=-1, keepdims=True) + RMS_EPS)
    return (y * gain.astype(jnp.float32)).astype(x.dtype)


def modulate(h, shift, scale):
    return h * (1.0 + scale[:, None, :]) + shift[:, None, :]


def short_conv_mixer(h, conv_buf, w_in, conv_w, w_out):
    seq = h.shape[1]
    b_gate, c_gate, xv = jnp.split(h @ w_in, 3, axis=-1)
    u = c_gate * xv
    u_ext = jnp.concatenate([conv_buf.astype(u.dtype), u], axis=1)
    conv = u_ext[:, 0:seq] * conv_w[0]
    for j in range(1, CONV_WIDTH):
        conv = conv + u_ext[:, j:j + seq] * conv_w[j]
    y = (b_gate * conv) @ w_out
    return y, u_ext[:, -(CONV_WIDTH - 1):, :]


def rwkv7_time_mix(h, shift_prev, wkv_prev, mu, w_r, w_k, w_v, w_o, w0, w1, w2,
                   a0, a1, a2, g1, g2, k_k, k_a, r_k, ln_w, ln_b):
    bsz, seq, _ = h.shape
    h_prev = jnp.concatenate([shift_prev.astype(h.dtype)[:, None, :], h[:, :-1, :]], axis=1)
    xx = h_prev - h
    xr, xw, xk, xv, xa, xg = (h + xx * mu[j] for j in range(N_SHIFT_MIX))
    r = xr @ w_r
    k = xk @ w_k
    v = xv @ w_v
    w_log = -jax.nn.softplus(-(w0 + jnp.tanh(xw @ w1) @ w2)) - 0.5
    a = jax.nn.sigmoid(a0 + (xa @ a1) @ a2)
    g = jax.nn.sigmoid(xg @ g1) @ g2

    def heads(t):
        return t.reshape(bsz, seq, N_HEADS, HEAD_SIZE).astype(jnp.float32)

    kk = heads(k * k_k)
    kk = kk * lax.rsqrt(jnp.maximum(jnp.sum(kk * kk, axis=-1, keepdims=True), 1e-24))
    k = k * (1.0 + (a - 1.0) * k_a)
    r_h, k_h, v_h, a_h = heads(r), heads(k), heads(v), heads(a)
    decay = jnp.exp(-jnp.exp(heads(w_log)))

    def step(S, inp):
        r_t, w_t, k_t, v_t, kk_t, a_t = inp
        sa = jnp.einsum('bhvk,bhk->bhv', S, -kk_t)
        S = (S * w_t[:, :, None, :]
             + sa[..., None] * (kk_t * a_t)[:, :, None, :]
             + v_t[..., None] * k_t[:, :, None, :])
        return S, jnp.einsum('bhvk,bhk->bhv', S, r_t)

    def tm(t):
        return jnp.swapaxes(t, 0, 1)

    wkv_new, o = lax.scan(step, wkv_prev.astype(jnp.float32),
                          (tm(r_h), tm(decay), tm(k_h), tm(v_h), tm(kk), tm(a_h)))
    o = tm(o)
    mean = jnp.mean(o, axis=-1, keepdims=True)
    var = jnp.mean(jnp.square(o - mean), axis=-1, keepdims=True)
    o = (o - mean) * lax.rsqrt(var + GN_EPS)
    o = (o * ln_w.astype(jnp.float32).reshape(N_HEADS, HEAD_SIZE)
         + ln_b.astype(jnp.float32).reshape(N_HEADS, HEAD_SIZE))
    o = o + jnp.sum(r_h * k_h * r_k.astype(jnp.float32), axis=-1, keepdims=True) * v_h
    y = (o.reshape(bsz, seq, D_MODEL).astype(h.dtype) * g) @ w_o
    return y, h[:, -1, :], wkv_new


def hier_moe(h, wg, bg, we, be, w1, w3, w2):
    bsz, seq, d = h.shape
    t = h.reshape(bsz * seq, d)
    lg = (t @ wg).astype(jnp.float32) + bg
    g_idx = jnp.argmax(lg, axis=-1)
    p_grp = jnp.take_along_axis(jax.nn.softmax(lg, axis=-1), g_idx[:, None], axis=-1)
    le = ((t @ we).astype(jnp.float32) + be).reshape(-1, N_GROUPS, EXPERTS_PER_GROUP)
    le_g = jnp.take_along_axis(le, g_idx[:, None, None], axis=1)[:, 0]
    top_v, top_i = lax.top_k(le_g, TOP_K_IN_GROUP)
    w_pair = p_grp * jax.nn.softmax(top_v, axis=-1)
    eid = g_idx[:, None] * EXPERTS_PER_GROUP + top_i
    gates = jnp.einsum('nke,nk->ne', jax.nn.one_hot(eid, N_EXPERTS, dtype=jnp.float32), w_pair)

    def body(acc, xs):
        w1e, w3e, w2e, ge = xs
        hid = jax.nn.silu(t @ w1e) * (t @ w3e)
        return acc + ge[:, None] * (hid @ w2e).astype(jnp.float32), None

    acc, _ = lax.scan(body, jnp.zeros(t.shape, jnp.float32), (w1, w3, w2, gates.T))
    return acc.reshape(bsz, seq, d).astype(h.dtype)


def trunk(x, c, conv_bufs, shift_bufs, wkv_bufs, p):
    new_conv, new_shift, new_wkv = [], [], []
    silu_c = jax.nn.silu(c)
    for i in range(DEPTH):
        mod = silu_c @ p['ada_w'][i] + p['ada_b'][i]
        sh_m, sc_m, gt_m, sh_f, sc_f, gt_f = jnp.split(mod, N_ADA, axis=-1)
        h = modulate(rms_norm(x, p['norm_g'][i, 0]), sh_m, sc_m)
        j = i // N_MIXERS
        if i % N_MIXERS == 0:
            y, buf = short_conv_mixer(h, conv_bufs[j], p['cv_in'][j], p['cv_w'][j], p['cv_out'][j])
            new_conv.append(buf)
        else:
            y, sbuf, st = rwkv7_time_mix(
                h, shift_bufs[j], wkv_bufs[j], p['rw_mu'][j], p['rw_r'][j], p['rw_k'][j],
                p['rw_v'][j], p['rw_o'][j], p['rw_w0'][j], p['rw_w1'][j], p['rw_w2'][j],
                p['rw_a0'][j], p['rw_a1'][j], p['rw_a2'][j], p['rw_g1'][j], p['rw_g2'][j],
                p['rw_kk'][j], p['rw_ka'][j], p['rw_rk'][j], p['rw_lnw'][j], p['rw_lnb'][j])
            new_shift.append(sbuf)
            new_wkv.append(st)
        x = x + gt_m[:, None, :] * y
        h = modulate(rms_norm(x, p['norm_g'][i, 1]), sh_f, sc_f)
        x = x + gt_f[:, None, :] * hier_moe(h, p['moe_wg'][i], p['moe_bg'][i], p['moe_we'][i],
                                            p['moe_be'][i], p['moe_w1'][i], p['moe_w3'][i],
                                            p['moe_w2'][i])
    y = rms_norm(x, p['final_g'])
    return y, jnp.stack(new_conv), jnp.stack(new_shift), jnp.stack(new_wkv)


def setup_inputs(seed: int = 0) -> dict:
    key = jax.random.key(seed)
    ks = iter(jax.random.split(key, 64))

    def nrm(shape, scale):
        return jax.random.normal(next(ks), shape, jnp.float32) * scale

    def uni(shape, lo, hi):
        return jax.random.uniform(next(ks), shape, jnp.float32, lo, hi)

    d, f, nh, hs = D_MODEL, D_EXPERT, N_HEADS, HEAD_SIZE
    nc, nr = N_CONV_LAYERS, N_RWKV_LAYERS
    return {
        'x_prompt': nrm((BATCH, SEQ, d), 1.0),
        'x_sample': nrm((DEC_BATCH, DEC_SEQ, d), 1.0),
        'c_prompt': nrm((BATCH, d), 1.0),
        'c_sample': nrm((DEC_BATCH, d), 1.0),
        'state_conv': nrm((nc, DEC_BATCH, CONV_WIDTH - 1, d), 0.5),
        'state_shift': nrm((nr, DEC_BATCH, d), 1.0),
        'state_wkv': nrm((nr, DEC_BATCH, nh, hs, hs), 0.5),
        'ada_w': nrm((DEPTH, d, N_ADA * d), 0.5 * d ** -0.5),
        'ada_b': nrm((DEPTH, N_ADA * d), 0.01),
        'norm_g': 1.0 + nrm((DEPTH, 2, d), 0.02),
        'final_g': 1.0 + nrm((d,), 0.02),
        'cv_in': nrm((nc, d, 3 * d), d ** -0.5),
        'cv_w': nrm((nc, CONV_WIDTH, d), CONV_WIDTH ** -0.5),
        'cv_out': nrm((nc, d, d), d ** -0.5),
        'rw_mu': uni((nr, N_SHIFT_MIX, d), 0.0, 1.0),
        'rw_r': nrm((nr, d, d), d ** -0.5),
        'rw_k': nrm((nr, d, d), d ** -0.5),
        'rw_v': nrm((nr, d, d), d ** -0.5),
        'rw_o': nrm((nr, d, d), d ** -0.5),
        'rw_w0': uni((nr, d), -6.0, -1.0),
        'rw_w1': nrm((nr, d, D_DECAY_LORA), 0.5 * d ** -0.5),
        'rw_w2': nrm((nr, D_DECAY_LORA, d), 0.5 * D_DECAY_LORA ** -0.5),
        'rw_a0': nrm((nr, d), 0.1),
        'rw_a1': nrm((nr, d, D_AAA_LORA), 0.5 * d ** -0.5),
        'rw_a2': nrm((nr, D_AAA_LORA, d), 0.5 * D_AAA_LORA ** -0.5),
        'rw_g1': nrm((nr, d, D_GATE_LORA), d ** -0.5),
        'rw_g2': nrm((nr, D_GATE_LORA, d), D_GATE_LORA ** -0.5),
        'rw_kk': 0.85 + nrm((nr, d), 0.02),
        'rw_ka': 1.0 + nrm((nr, d), 0.02),
        'rw_rk': nrm((nr, nh, hs), 0.1),
        'rw_lnw': 1.0 + nrm((nr, d), 0.02),
        'rw_lnb': nrm((nr, d), 0.01),
        'moe_wg': nrm((DEPTH, d, N_GROUPS), d ** -0.5),
        'moe_bg': nrm((DEPTH, N_GROUPS), 0.01),
        'moe_we': nrm((DEPTH, d, N_EXPERTS), d ** -0.5),
        'moe_be': nrm((DEPTH, N_EXPERTS), 0.01),
        'moe_w1': nrm((DEPTH, N_EXPERTS, d, f), d ** -0.5),
        'moe_w3': nrm((DEPTH, N_EXPERTS, d, f), d ** -0.5),
        'moe_w2': nrm((DEPTH, N_EXPERTS, f, d), f ** -0.5),
    }


def reference(x_prompt, x_sample, c_prompt, c_sample, state_conv, state_shift, state_wkv,
              ada_w, ada_b, norm_g, final_g, cv_in, cv_w, cv_out,
              rw_mu, rw_r, rw_k, rw_v, rw_o, rw_w0, rw_w1, rw_w2, rw_a0, rw_a1, rw_a2,
              rw_g1, rw_g2, rw_kk, rw_ka, rw_rk, rw_lnw, rw_lnb,
              moe_wg, moe_bg, moe_we, moe_be, moe_w1, moe_w3, moe_w2):
    p = dict(ada_w=ada_w, ada_b=ada_b, norm_g=norm_g, final_g=final_g,
             cv_in=cv_in, cv_w=cv_w, cv_out=cv_out,
             rw_mu=rw_mu, rw_r=rw_r, rw_k=rw_k, rw_v=rw_v, rw_o=rw_o,
             rw_w0=rw_w0, rw_w1=rw_w1, rw_w2=rw_w2, rw_a0=rw_a0, rw_a1=rw_a1, rw_a2=rw_a2,
             rw_g1=rw_g1, rw_g2=rw_g2, rw_kk=rw_kk, rw_ka=rw_ka, rw_rk=rw_rk,
             rw_lnw=rw_lnw, rw_lnb=rw_lnb,
             moe_wg=moe_wg, moe_bg=moe_bg, moe_we=moe_we, moe_be=moe_be,
             moe_w1=moe_w1, moe_w3=moe_w3, moe_w2=moe_w2)
    bp = x_prompt.shape[0]
    conv0 = jnp.zeros((N_CONV_LAYERS, bp, CONV_WIDTH - 1, D_MODEL), x_prompt.dtype)
    shift0 = jnp.zeros((N_RWKV_LAYERS, bp, D_MODEL), x_prompt.dtype)
    wkv0 = jnp.zeros((N_RWKV_LAYERS, bp, N_HEADS, HEAD_SIZE, HEAD_SIZE), jnp.float32)
    y_prompt, conv_prompt, shift_prompt, wkv_prompt = trunk(x_prompt, c_prompt, conv0, shift0, wkv0, p)
    y_sample, conv_sample, shift_sample, wkv_sample = trunk(x_sample, c_sample, state_conv,
                                                            state_shift, state_wkv, p)
    return (y_prompt, y_sample, conv_prompt, shift_prompt, wkv_prompt,
            conv_sample, shift_sample, wkv_sample)
```

```python
import functools

import jax
import jax.numpy as jnp
from jax import lax
from jax.experimental import pallas as pl
from jax.experimental.pallas import tpu as pltpu

f32 = jnp.float32
bf16 = jnp.bfloat16
i32 = jnp.int32
u32 = jnp.uint32

D_MODEL = 2048
HEAD_SIZE = 64
N_HEADS = D_MODEL // HEAD_SIZE
PAIR = 2 * HEAD_SIZE
N_PAIRS = D_MODEL // PAIR
CONV_WIDTH = 3
N_GROUPS = 4
EXPERTS_PER_GROUP = 4
N_EXPERTS = N_GROUPS * EXPERTS_PER_GROUP
D_EXPERT = D_MODEL // 4
N_ADA = 6
N_SHIFT_MIX = 6
RMS_EPS = 1e-6
GN_EPS = 64e-5
LORA_PAD = 128
N_CLASSES = N_GROUPS * 6
CLASS_ROWS = 32
ROUTER_ROWS = 128
HALF = D_MODEL // 2
ROW_WORDS = HALF + 128
WKV_CHUNK = 64
CARRY_ROWS = 8
TAIL_ROWS = 8
TAIL_WINDOW = 2 * TAIL_ROWS

V7X_VMEM_LIMIT = 56 * 1024 * 1024
MOE_TILE = 512


def _cparams(sem, vmem=V7X_VMEM_LIMIT):
    return pltpu.CompilerParams(dimension_semantics=sem, vmem_limit_bytes=vmem)


def _dot(a, b):
    return jnp.dot(a, b, preferred_element_type=f32)


def _dot_nt(a, b):
    return lax.dot_general(a, b, (((1,), (1,)), ((), ())), preferred_element_type=f32)


def _dot_tn(a, b):
    return lax.dot_general(a, b, (((0,), (0,)), ((), ())), preferred_element_type=f32)


def _split2(x):
    hi = x.astype(bf16)
    lo = (x - hi.astype(f32)).astype(bf16)
    return hi, lo


def _sigmoid(x):
    return 1.0 / (1.0 + jnp.exp(-x))


def _pack_rows(x):
    hi = pltpu.bitcast(x[:, :HALF].astype(bf16).astype(f32), u32)
    lo = pltpu.bitcast(x[:, HALF:].astype(bf16).astype(f32), u32)
    return (hi & jnp.uint32(0xFFFF0000)) | (lo >> 16)


def _unpack_rows(p):
    left = pltpu.bitcast(p & jnp.uint32(0xFFFF0000), f32)
    right = pltpu.bitcast(p << 16, f32)
    return left, right


def _dot3(a_hi, a_lo, w_hi, w_lo):
    return _dot(a_hi, w_hi) + _dot(a_lo, w_hi) + _dot(a_hi, w_lo)


def _ada_body(c_ref, w_ref, b_ref, o_ref):
    c = c_ref[...]
    s_hi, s_lo = _split2(c * _sigmoid(c))
    w_hi, w_lo = _split2(w_ref[0])
    o_ref[0] = _dot3(s_hi, s_lo, w_hi, w_lo) + b_ref[0]


def _ada(c_all, ada_w, ada_b, tn=1024):
    depth, d, n = ada_w.shape
    rows = c_all.shape[0]
    return pl.pallas_call(
        _ada_body,
        grid=(depth, n // tn),
        in_specs=[
            pl.BlockSpec((rows, d), lambda l, j: (0, 0)),
            pl.BlockSpec((1, d, tn), lambda l, j: (l, 0, j)),
            pl.BlockSpec((1, 1, tn), lambda l, j: (l, 0, j)),
        ],
        out_specs=pl.BlockSpec((1, rows, tn), lambda l, j: (l, 0, j)),
        out_shape=jax.ShapeDtypeStruct((depth, rows, n), f32),
        compiler_params=_cparams(("arbitrary", "arbitrary")),
        name="ada",
    )(c_all, ada_w, ada_b.reshape(depth, 1, n))


def _norm_mod(x, gain, sh, sc):
    ms = jnp.mean(x * x, axis=-1, keepdims=True)
    y = x * lax.rsqrt(ms + RMS_EPS) * gain
    return y * (1.0 + sc) + sh


def _conv_in_body(x_ref, sh_ref, sc_ref, g_ref, cw_ref, st_ref, *rest, bs, tt, tn, precise):
    nw = 6 if precise else 3
    w_refs = rest[:nw]
    z_ref, so_ref, h_scr, carry_scr, uext_scr = rest[nw:]
    t = pl.program_id(1)
    j = pl.program_id(2)

    @pl.when(j == 0)
    def _():
        h = _norm_mod(x_ref[...], g_ref[...], sh_ref[...], sc_ref[...]).reshape(bs * tt, D_MODEL)
        if precise:
            h_scr[0], h_scr[1] = _split2(h)
        else:
            h_scr[0] = h.astype(bf16)

    @pl.when(t == 0)
    def _():
        carry_scr[j] = st_ref[...]

    if precise:
        b_gate, c_gate, xv = (_dot3(h_scr[0], h_scr[1], w_refs[i][...], w_refs[i + 3][...]) for i in range(3))
    else:
        b_gate, c_gate, xv = (_dot(h_scr[0], w_refs[i][...]) for i in range(3))
    u = (c_gate * xv).reshape(bs, tt, tn)
    uext_scr[:, 0:CARRY_ROWS, :] = carry_scr[j]
    uext_scr[:, CARRY_ROWS:, :] = u
    cw = cw_ref[...]
    conv = (uext_scr[:, CARRY_ROWS - 2:CARRY_ROWS - 2 + tt, :] * cw[0]
            + uext_scr[:, CARRY_ROWS - 1:CARRY_ROWS - 1 + tt, :] * cw[1]
            + u * cw[2])
    z_ref[...] = (b_gate.reshape(bs, tt, tn) * conv).astype(z_ref.dtype)
    last = uext_scr[:, tt:tt + CARRY_ROWS, :]
    carry_scr[j] = last
    so_ref[:, 0] = last


def _conv_in(x, sh, sc, gain, w_in_hi, conv_w, state8, *, bs, tt, tn=512, w_in_lo=None):
    bsz, seq, d = x.shape
    nj = d // tn
    precise = w_in_lo is not None
    body = functools.partial(_conv_in_body, bs=bs, tt=tt, tn=tn, precise=precise)
    w_specs = [pl.BlockSpec((d, tn), lambda b, t, j, k=k: (0, j + k * nj)) for k in range(3)]
    weights = [w_in_hi] * 3 + ([w_in_lo] * 3 if precise else [])
    return pl.pallas_call(
        body,
        grid=(bsz // bs, seq // tt, nj),
        in_specs=[
            pl.BlockSpec((bs, tt, d), lambda b, t, j: (b, t, 0)),
            pl.BlockSpec((bs, 1, d), lambda b, t, j: (b, 0, 0)),
            pl.BlockSpec((bs, 1, d), lambda b, t, j: (b, 0, 0)),
            pl.BlockSpec((1, d), lambda b, t, j: (0, 0)),
            pl.BlockSpec((CONV_WIDTH, tn), lambda b, t, j: (0, j)),
            pl.BlockSpec((bs, CARRY_ROWS, tn), lambda b, t, j: (b, 0, j)),
        ] + w_specs * (2 if precise else 1),
        out_specs=[
            pl.BlockSpec((bs, tt, tn), lambda b, t, j: (b, t, j)),
            pl.BlockSpec((bs, 1, CARRY_ROWS, tn), lambda b, t, j: (b, t, 0, j)),
        ],
        out_shape=[
            jax.ShapeDtypeStruct((bsz, seq, d), f32 if precise else bf16),
            jax.ShapeDtypeStruct((bsz, seq // tt, CARRY_ROWS, d), f32),
        ],
        scratch_shapes=[
            pltpu.VMEM((2 if precise else 1, bs * tt, d), bf16),
            pltpu.VMEM((nj, bs, CARRY_ROWS, tn), f32),
            pltpu.VMEM((bs, tt + CARRY_ROWS, tn), f32),
        ],
        compiler_params=_cparams(("arbitrary", "arbitrary", "arbitrary")),
        name="conv_in_precise" if precise else "conv_in",
    )(x, sh, sc, gain, conv_w, state8, *weights)


def _tail_out_body(z_ref, x_ref, gt_ref, wh_ref, wl_ref, o_ref, *, bs, tt):
    z_hi, z_lo = _split2(z_ref[...].reshape(bs * tt, D_MODEL))
    y = _dot3(z_hi, z_lo, wh_ref[...], wl_ref[...])
    tn = y.shape[1]
    o_ref[...] = x_ref[...] + gt_ref[...] * y.reshape(bs, tt, tn)


def _tail_out(z, x, gt, w_hi, w_lo, *, tn=512):
    bsz, tt, d = x.shape
    body = functools.partial(_tail_out_body, bs=bsz, tt=tt)
    return pl.pallas_call(
        body,
        grid=(d // tn,),
        in_specs=[
            pl.BlockSpec((bsz, tt, d), lambda j: (0, 0, 0)),
            pl.BlockSpec((bsz, tt, tn), lambda j: (0, 0, j)),
            pl.BlockSpec((bsz, 1, tn), lambda j: (0, 0, j)),
            pl.BlockSpec((d, tn), lambda j: (0, j)),
            pl.BlockSpec((d, tn), lambda j: (0, j)),
        ],
        out_specs=pl.BlockSpec((bsz, tt, tn), lambda j: (0, 0, j)),
        out_shape=jax.ShapeDtypeStruct((bsz, tt, d), f32),
        compiler_params=_cparams(("arbitrary",)),
        name="tail_out",
    )(z, x, gt, w_hi, w_lo)


def _route_rows(logit):
    lg = [logit[g:g + 1, :] for g in range(N_GROUPS)]
    le = [logit[N_GROUPS + e:N_GROUPS + e + 1, :] for e in range(N_EXPERTS)]
    gmax = jnp.maximum(jnp.maximum(lg[0], lg[1]), jnp.maximum(lg[2], lg[3]))
    gidx = jnp.where(lg[0] == gmax, 0, jnp.where(lg[1] == gmax, 1, jnp.where(lg[2] == gmax, 2, 3)))
    denom = (jnp.exp(lg[0] - gmax) + jnp.exp(lg[1] - gmax)
             + jnp.exp(lg[2] - gmax) + jnp.exp(lg[3] - gmax))
    p_grp = 1.0 / denom
    leg = [jnp.where(gidx == 0, le[i],
                     jnp.where(gidx == 1, le[EXPERTS_PER_GROUP + i],
                               jnp.where(gidx == 2, le[2 * EXPERTS_PER_GROUP + i],
                                         le[3 * EXPERTS_PER_GROUP + i])))
           for i in range(EXPERTS_PER_GROUP)]
    v1 = jnp.maximum(jnp.maximum(leg[0], leg[1]), jnp.maximum(leg[2], leg[3]))
    i1 = jnp.where(leg[0] == v1, 0, jnp.where(leg[1] == v1, 1, jnp.where(leg[2] == v1, 2, 3)))
    neg = jnp.float32(-jnp.inf)
    rest = [jnp.where(i1 == i, neg, leg[i]) for i in range(EXPERTS_PER_GROUP)]
    v2 = jnp.maximum(jnp.maximum(rest[0], rest[1]), jnp.maximum(rest[2], rest[3]))
    i2 = jnp.where((rest[0] == v2) & (i1 != 0), 0,
                   jnp.where((rest[1] == v2) & (i1 != 1), 1,
                             jnp.where((rest[2] == v2) & (i1 != 2), 2, 3)))
    s = jnp.exp(v2 - v1)
    w_first = p_grp / (1.0 + s)
    w_second = p_grp * s / (1.0 + s)
    i_lo = jnp.minimum(i1, i2)
    i_hi = jnp.maximum(i1, i2)
    pair_base = jnp.where(i_lo == 0, 0, jnp.where(i_lo == 1, 3, 5))
    cls = gidx * 6 + pair_base + (i_hi - i_lo - 1)
    g_lo = jnp.where(i1 < i2, w_first, w_second)
    g_hi = jnp.where(i1 < i2, w_second, w_first)
    return cls, g_lo, g_hi


def _out_router_body(a_ref, w_ref, x_ref, gt_ref, g_ref, sh_ref, sc_ref, wrt_ref, rb_ref,
                     cin_ref, tri_ref, *rest, bs, tt, with_tail):
    if with_tail:
        tail_ref, xo_ref, rows_ref, info_ref, cout_ref, cnt_scr = rest
    else:
        xo_ref, rows_ref, info_ref, cout_ref, cnt_scr = rest
    tm = bs * tt
    first = (pl.program_id(0) == 0) & (pl.program_id(1) == 0)

    @pl.when(first)
    def _():
        cnt_scr[...] = cin_ref[...]

    y = _dot(a_ref[...].reshape(tm, D_MODEL), w_ref[...])
    xo_ref[...] = x_ref[...] + gt_ref[...] * y.reshape(bs, tt, D_MODEL)
    if with_tail:
        @pl.when(pl.program_id(1) == pl.num_programs(1) - 1)
        def _():
            xo_ref[:, tt - TAIL_ROWS:, :] = tail_ref[...]
    xn = xo_ref[...]
    h = _norm_mod(xn, g_ref[...], sh_ref[...], sc_ref[...]).reshape(tm, D_MODEL)
    h_hi, h_lo = _split2(h)
    wrt = wrt_ref[...]
    p_hi = _dot_nt(wrt, h_hi)
    p_lo = _dot_nt(wrt[:ROUTER_ROWS], h_lo)
    logit = p_hi[:ROUTER_ROWS] + p_hi[ROUTER_ROWS:] + p_lo + rb_ref[...]
    cls, g_lo, g_hi = _route_rows(logit)

    crow = lax.broadcasted_iota(i32, (CLASS_ROWS, tm), 0)
    onehot = (crow == cls).astype(f32)
    before = _dot(onehot.astype(bf16), tri_ref[...])
    base = cnt_scr[:, 0:1]
    rank = jnp.sum(onehot * (before + base), axis=0, keepdims=True).astype(i32)
    cnt_new = cnt_scr[...] + jnp.sum(onehot, axis=1, keepdims=True)
    cnt_scr[...] = cnt_new
    cout_ref[...] = cnt_new

    irow = lax.broadcasted_iota(i32, (8, tm), 0)
    info_ref[0] = jnp.where(irow == 0, cls, jnp.where(irow == 1, rank, 0))

    grow = lax.broadcasted_iota(i32, (ROUTER_ROWS, tm), 0)
    gates_t = jnp.where(grow == 0, g_lo, jnp.where(grow == 1, g_hi, 0.0))
    gates = pltpu.bitcast(gates_t.T, u32)
    rows_ref[:, :, :HALF] = _pack_rows(h).reshape(bs, tt, HALF)
    rows_ref[:, :, HALF:] = gates.reshape(bs, tt, ROW_WORDS - HALF)


def _out_router(a, w_bf, x, gt, gain, sh, sc, wrt, rbias, cnt_in, tri, *, bs, tt, tail=None):
    bsz, seq, d = x.shape
    tm = bs * tt
    nt = seq // tt
    ntiles = (bsz // bs) * nt
    with_tail = tail is not None
    body = functools.partial(_out_router_body, bs=bs, tt=tt, with_tail=with_tail)
    tail_specs = [pl.BlockSpec((bs, TAIL_ROWS, d), lambda b, t: (b, 0, 0))] if with_tail else []
    tail_args = [tail] if with_tail else []
    return pl.pallas_call(
        body,
        grid=(bsz // bs, nt),
        in_specs=[
            pl.BlockSpec((bs, tt, d), lambda b, t: (b, t, 0)),
            pl.BlockSpec((d, d), lambda b, t: (0, 0)),
            pl.BlockSpec((bs, tt, d), lambda b, t: (b, t, 0)),
            pl.BlockSpec((bs, 1, d), lambda b, t: (b, 0, 0)),
            pl.BlockSpec((1, d), lambda b, t: (0, 0)),
            pl.BlockSpec((bs, 1, d), lambda b, t: (b, 0, 0)),
            pl.BlockSpec((bs, 1, d), lambda b, t: (b, 0, 0)),
            pl.BlockSpec((2 * ROUTER_ROWS, d), lambda b, t: (0, 0)),
            pl.BlockSpec((ROUTER_ROWS, 1), lambda b, t: (0, 0)),
            pl.BlockSpec((CLASS_ROWS, 128), lambda b, t: (0, 0)),
            pl.BlockSpec((tm, tm), lambda b, t: (0, 0)),
        ] + tail_specs,
        out_specs=[
            pl.BlockSpec((bs, tt, d), lambda b, t: (b, t, 0)),
            pl.BlockSpec((bs, tt, ROW_WORDS), lambda b, t: (b, t, 0)),
            pl.BlockSpec((1, 8, tm), lambda b, t: (b * nt + t, 0, 0)),
            pl.BlockSpec((CLASS_ROWS, 128), lambda b, t: (0, 0)),
        ],
        out_shape=[
            jax.ShapeDtypeStruct((bsz, seq, d), f32),
            jax.ShapeDtypeStruct((bsz, seq, ROW_WORDS), u32),
            jax.ShapeDtypeStruct((ntiles, 8, tm), i32),
            jax.ShapeDtypeStruct((CLASS_ROWS, 128), f32),
        ],
        scratch_shapes=[pltpu.VMEM((CLASS_ROWS, 128), f32)],
        compiler_params=_cparams(("arbitrary", "arbitrary")),
        name="out_router",
    )(a, w_bf, x, gt, gain, sh, sc, wrt, rbias, cnt_in, tri, *tail_args)


def _row_copy_scatter(src_ref, dst_ref, pos_ref, sem, r):
    return pltpu.make_async_copy(src_ref.at[pl.ds(r, 1), :], dst_ref.at[pl.ds(pos_ref[0, 0, r], 1), :], sem)


def _dispatch_body(pos_ref, rows_ref, dst_in_ref, dst_ref, sem, *, tm):
    del dst_in_ref

    def start(r, carry):
        _row_copy_scatter(rows_ref, dst_ref, pos_ref, sem, r).start()
        return carry

    lax.fori_loop(0, tm, start, 0)

    def wait(r, carry):
        _row_copy_scatter(rows_ref, dst_ref, pos_ref, sem, r).wait()
        return carry

    lax.fori_loop(0, tm, wait, 0)


def _dispatch(rows, pos3, sorted_rows, *, tm):
    n = rows.shape[0]
    body = functools.partial(_dispatch_body, tm=tm)
    return pl.pallas_call(
        body,
        grid=(n // tm,),
        in_specs=[
            pl.BlockSpec((1, 1, tm), lambda i: (i, 0, 0), memory_space=pltpu.SMEM),
            pl.BlockSpec((tm, ROW_WORDS), lambda i: (i, 0)),
            pl.BlockSpec(memory_space=pl.ANY),
        ],
        out_specs=pl.BlockSpec(memory_space=pl.ANY),
        out_shape=jax.ShapeDtypeStruct(sorted_rows.shape, sorted_rows.dtype),
        scratch_shapes=[pltpu.SemaphoreType.DMA(())],
        input_output_aliases={2: 0},
        compiler_params=_cparams(("arbitrary",)),
        name="moe_dispatch",
    )(pos3, rows, sorted_rows)


def _moe_body(ea_ref, eb_ref, valid_ref, xs_ref, w1a_ref, w3a_ref, w2a_ref, w1b_ref, w3b_ref, w2b_ref,
              ys_ref):
    del ea_ref, eb_ref
    i = pl.program_id(0)

    @pl.when(valid_ref[i] > 0)
    def _():
        left, right = _unpack_rows(xs_ref[:, :HALF])
        x = jnp.concatenate([left.astype(bf16), right.astype(bf16)], axis=1)
        gates = pltpu.bitcast(xs_ref[:, HALF:], f32)

        def expert(w1_ref, w3_ref, w2_ref, gate):
            h1 = _dot(x, w1_ref[0])
            h3 = _dot(x, w3_ref[0])
            hid = (h1 * _sigmoid(h1)) * h3 * gate
            return _dot(hid.astype(bf16), w2_ref[0])

        y = expert(w1a_ref, w3a_ref, w2a_ref, gates[:, 0:1]) + expert(w1b_ref, w3b_ref, w2b_ref, gates[:, 1:2])
        ys_ref[...] = _pack_rows(y)

    @pl.when(valid_ref[i] == 0)
    def _():
        ys_ref[...] = jnp.zeros(ys_ref.shape, u32)


def _moe(sorted_rows, tile_ea, tile_eb, tile_valid, w1_bf, w3_bf, w2_bf, *, tm):
    p = sorted_rows.shape[0]
    d, f = D_MODEL, D_EXPERT

    def wa(i, ea, eb, valid):
        return (ea[i], 0, 0)

    def wb(i, ea, eb, valid):
        return (eb[i], 0, 0)

    grid_spec = pltpu.PrefetchScalarGridSpec(
        num_scalar_prefetch=3,
        grid=(p // tm,),
        in_specs=[
            pl.BlockSpec((tm, ROW_WORDS), lambda i, ea, eb, valid: (i, 0)),
            pl.BlockSpec((1, d, f), wa), pl.BlockSpec((1, d, f), wa), pl.BlockSpec((1, f, d), wa),
            pl.BlockSpec((1, d, f), wb), pl.BlockSpec((1, d, f), wb), pl.BlockSpec((1, f, d), wb),
        ],
        out_specs=pl.BlockSpec((tm, HALF), lambda i, ea, eb, valid: (i, 0)),
    )
    return pl.pallas_call(
        _moe_body,
        grid_spec=grid_spec,
        out_shape=jax.ShapeDtypeStruct((p, HALF), u32),
        compiler_params=_cparams(("arbitrary",)),
        name="moe_experts",
    )(tile_ea, tile_eb, tile_valid, sorted_rows, w1_bf, w3_bf, w2_bf, w1_bf, w3_bf, w2_bf)


def _row_copy_gather(src_ref, dst_ref, pos_ref, sem, r):
    return pltpu.make_async_copy(src_ref.at[pl.ds(pos_ref[0, 0, r], 1), :], dst_ref.at[pl.ds(r, 1), :], sem)


def _combine_body(pos_ref, pos_next_ref, ys_ref, x_ref, gt_ref, fg_ref, o_ref, buf, sems, *, bs, tt, final_norm):
    tm = bs * tt
    nt = pl.num_programs(1)
    step = pl.program_id(0) * nt + pl.program_id(1)
    nsteps = pl.num_programs(0) * nt
    slot = step % 2

    def issue(p_ref, s):
        def start(r, carry):
            _row_copy_gather(ys_ref, buf.at[s], p_ref, sems.at[s], r).start()
            return carry
        lax.fori_loop(0, tm, start, 0)

    @pl.when(step == 0)
    def _():
        issue(pos_ref, 0)

    @pl.when(step + 1 < nsteps)
    def _():
        issue(pos_next_ref, 1 - slot)

    def wait(r, carry):
        _row_copy_gather(ys_ref, buf.at[slot], pos_ref, sems.at[slot], r).wait()
        return carry

    lax.fori_loop(0, tm, wait, 0)

    left, right = _unpack_rows(buf[slot])
    y = jnp.concatenate([left, right], axis=1).reshape(bs, tt, D_MODEL)
    xn = x_ref[...] + gt_ref[...] * y
    if final_norm:
        ms = jnp.mean(xn * xn, axis=-1, keepdims=True)
        xn = xn * lax.rsqrt(ms + RMS_EPS) * fg_ref[...]
    o_ref[...] = xn


def _combine(ys, pos3, x, gt, final_g, *, bs, tt, final_norm):
    bsz, seq, d = x.shape
    tm = bs * tt
    nt = seq // tt
    nsteps = (bsz // bs) * nt
    body = functools.partial(_combine_body, bs=bs, tt=tt, final_norm=final_norm)
    return pl.pallas_call(
        body,
        grid=(bsz // bs, nt),
        in_specs=[
            pl.BlockSpec((1, 1, tm), lambda b, t: (b * nt + t, 0, 0), memory_space=pltpu.SMEM),
            pl.BlockSpec((1, 1, tm), lambda b, t: (jnp.minimum(b * nt + t + 1, nsteps - 1), 0, 0),
                         memory_space=pltpu.SMEM),
            pl.BlockSpec(memory_space=pl.ANY),
            pl.BlockSpec((bs, tt, d), lambda b, t: (b, t, 0)),
            pl.BlockSpec((bs, 1, d), lambda b, t: (b, 0, 0)),
            pl.BlockSpec((1, d), lambda b, t: (0, 0)),
        ],
        out_specs=pl.BlockSpec((bs, tt, d), lambda b, t: (b, t, 0)),
        out_shape=jax.ShapeDtypeStruct((bsz, seq, d), f32),
        scratch_shapes=[pltpu.VMEM((2, tm, HALF), u32), pltpu.SemaphoreType.DMA((2,))],
        compiler_params=_cparams(("arbitrary", "arbitrary")),
        name="moe_combine",
    )(pos3, pos3, ys, x, gt, final_g)


def _seg_sum(x, ones_bd):
    hi, lo = _split2(x)
    return _dot(hi, ones_bd) + _dot(lo, ones_bd)


def _rwkv_proj_body(x_ref, sh_ref, sc_ref, g_ref, mu_ref, shift_ref,
                    wr_ref, wk_ref, wv_ref, w1_ref, a1_ref, g1_ref, w2_ref, a2_ref, g2_ref,
                    vec_ref, ones_ref,
                    r_o, k_o, v_o, kk_o, b_o, g_o, lw_o, sh_o,
                    hs_scr, mix_scr, l1w_scr, l1a_scr, l1g_scr, *, bs, tt, tn):
    tm = bs * tt
    t = pl.program_id(1)
    j = pl.program_id(2)
    prev_row = CARRY_ROWS - 1

    @pl.when(j == 0)
    def _():
        @pl.when(t == 0)
        def _():
            hs_scr[:, prev_row:CARRY_ROWS, :] = shift_ref[...]

        h = _norm_mod(x_ref[...], g_ref[...], sh_ref[...], sc_ref[...])
        hs_scr[:, CARRY_ROWS:, :] = h
        h_prev = hs_scr[:, prev_row:prev_row + tt, :]
        xx = h_prev - h
        for m in range(N_SHIFT_MIX):
            mix_scr[m] = (h + xx * mu_ref[m]).reshape(tm, D_MODEL).astype(bf16)
        last = hs_scr[:, prev_row + tt:CARRY_ROWS + tt, :]
        hs_scr[:, prev_row:CARRY_ROWS, :] = last
        sh_o[...] = last
        l1w_scr[...] = jnp.tanh(_dot(mix_scr[1], w1_ref[...])).astype(bf16)
        l1a_scr[...] = _dot(mix_scr[4], a1_ref[...]).astype(bf16)
        l1g_scr[...] = _sigmoid(_dot(mix_scr[5], g1_ref[...])).astype(bf16)

    vec = vec_ref[...]
    w0, a0, k_k, k_a = vec[0], vec[1], vec[2], vec[3]
    r = _dot(mix_scr[0], wr_ref[...])
    k = _dot(mix_scr[2], wk_ref[...])
    v = _dot(mix_scr[3], wv_ref[...])
    wl = w0 + _dot(l1w_scr[...], w2_ref[...])
    a = _sigmoid(a0 + _dot(l1a_scr[...], a2_ref[...]))
    g = _dot(l1g_scr[...], g2_ref[...])
    neg = -wl
    softplus = jnp.maximum(neg, 0.0) + jnp.log(1.0 + jnp.exp(-jnp.abs(neg)))
    w_log = -softplus - 0.5
    lw = -jnp.exp(w_log)
    kkr = k * k_k
    ss = _seg_sum(kkr * kkr, ones_ref[...])
    kk = kkr * lax.rsqrt(jnp.maximum(ss, 1e-24))
    k2 = k * (1.0 + (a - 1.0) * k_a)
    shp = (bs, tt, tn)
    r_o[...] = r.reshape(shp).astype(bf16)
    k_o[...] = k2.reshape(shp).astype(bf16)
    v_o[...] = v.reshape(shp).astype(bf16)
    kk_o[...] = kk.reshape(shp).astype(bf16)
    b_o[...] = (kk * a).reshape(shp).astype(bf16)
    g_o[...] = g.reshape(shp).astype(bf16)
    lw_o[...] = lw.reshape(shp)


def _rwkv_proj(x, sh, sc, gain, mu, shift, wr, wk, wv, w1p, a1p, g1, w2p, a2p, g2, vec, ones_bd,
               *, bs, tt, tn=256):
    bsz, seq, d = x.shape
    tm = bs * tt
    dg = g1.shape[1]
    body = functools.partial(_rwkv_proj_body, bs=bs, tt=tt, tn=tn)
    const2 = lambda b, t, j: (0, 0)
    colblk = lambda b, t, j: (0, j)
    tok = lambda b, t, j: (b, t, j)
    act = lambda dt: jax.ShapeDtypeStruct((bsz, seq, d), dt)
    return pl.pallas_call(
        body,
        grid=(bsz // bs, seq // tt, d // tn),
        in_specs=[
            pl.BlockSpec((bs, tt, d), lambda b, t, j: (b, t, 0)),
            pl.BlockSpec((bs, 1, d), lambda b, t, j: (b, 0, 0)),
            pl.BlockSpec((bs, 1, d), lambda b, t, j: (b, 0, 0)),
            pl.BlockSpec((1, d), const2),
            pl.BlockSpec((N_SHIFT_MIX, d), const2),
            pl.BlockSpec((bs, 1, d), lambda b, t, j: (b, 0, 0)),
            pl.BlockSpec((d, tn), colblk), pl.BlockSpec((d, tn), colblk), pl.BlockSpec((d, tn), colblk),
            pl.BlockSpec((d, LORA_PAD), const2), pl.BlockSpec((d, LORA_PAD), const2),
            pl.BlockSpec((d, dg), const2),
            pl.BlockSpec((LORA_PAD, tn), colblk), pl.BlockSpec((LORA_PAD, tn), colblk),
            pl.BlockSpec((dg, tn), colblk),
            pl.BlockSpec((8, tn), colblk),
            pl.BlockSpec((tn, tn), const2),
        ],
        out_specs=[pl.BlockSpec((bs, tt, tn), tok)] * 7 + [pl.BlockSpec((bs, 1, d), lambda b, t, j: (b, 0, 0))],
        out_shape=[act(bf16)] * 6 + [act(f32), jax.ShapeDtypeStruct((bsz, 1, d), f32)],
        scratch_shapes=[
            pltpu.VMEM((bs, tt + CARRY_ROWS, d), f32),
            pltpu.VMEM((N_SHIFT_MIX, tm, d), bf16),
            pltpu.VMEM((tm, LORA_PAD), bf16),
            pltpu.VMEM((tm, LORA_PAD), bf16),
            pltpu.VMEM((tm, dg), bf16),
        ],
        compiler_params=_cparams(("arbitrary", "arbitrary", "arbitrary")),
        name="rwkv_proj",
    )(x, sh, sc, gain, mu, shift, wr, wk, wv, w1p, a1p, g1, w2p, a2p, g2, vec, ones_bd)


def _block_diag_rows(x, head0):
    return jnp.concatenate([jnp.where(head0, x, 0.0), jnp.where(head0, 0.0, x)], axis=0)


def _wkv_chunk(r, k, v, kk, beta, lw, c, state, masks):
    head0, strict, incl, eye, bd_mask = masks
    n = WKV_CHUNK
    c_end = c[n - 1:n, :]
    g_in = jnp.exp(c)
    g_ex = jnp.exp(c - lw)
    g_inv = jnp.exp(-c)
    g_end = jnp.exp(c_end - c)
    a_t = -(kk * g_ex)
    r_t = r * g_in
    ar = jnp.concatenate([a_t, r_t], axis=0).astype(bf16)
    bd = lambda x: _block_diag_rows(x, head0).astype(bf16)
    a1 = _dot_nt(ar, bd(beta * g_inv))
    a2 = _dot_nt(ar, bd(k * g_inv))
    a_ab = jnp.where(strict, a1[:n], 0.0)
    a_rb = jnp.where(incl, a1[n:], 0.0)
    a_ak = jnp.where(strict, a2[:n], 0.0)
    a_rk = jnp.where(incl, a2[n:], 0.0)
    q = a_ab
    p = eye + q
    q = _dot(q.astype(bf16), bd(q))
    for level in range(5):
        if level < 4:
            res = _dot(jnp.concatenate([p, q], axis=0).astype(bf16), bd(q))
            p = p + res[:n]
            q = res[n:]
        else:
            p = p + _dot(p.astype(bf16), bd(q))
    s_bf = state.astype(bf16)
    v_bd = bd(v)
    rhs = _dot_nt(ar[:n], s_bf) + _dot(a_ak.astype(bf16), v_bd)
    u = _dot(p.astype(bf16), bd(rhs))
    o = _dot_nt(ar[n:], s_bf) + _dot(a_rb.astype(bf16), bd(u)) + _dot(a_rk.astype(bf16), v_bd)
    uv = jnp.concatenate([u, v], axis=0).astype(bf16)
    bk = jnp.concatenate([beta * g_end, k * g_end], axis=0).astype(bf16)
    new_state = state * jnp.exp(c_end) + jnp.where(bd_mask, _dot_tn(uv, bk), 0.0)
    return o, new_state


def _wkv_body(r_ref, k_ref, v_ref, kk_ref, b_ref, g_ref, lw_ref, s0_ref, vec_ref, tril_ref, ones_ref,
              z_ref, so_ref, s_scr, *, bs, tc, npair):
    t = pl.program_id(2)
    n = WKV_CHUNK

    @pl.when(t == 0)
    def _():
        s_scr[...] = s0_ref[...]

    lane = lax.broadcasted_iota(i32, (n, PAIR), 1)
    row = lax.broadcasted_iota(i32, (n, PAIR), 0)
    head0 = lane < HEAD_SIZE
    within = lane & (HEAD_SIZE - 1)
    strict = within < row
    incl = within <= row
    eye = (within == row).astype(f32)
    bd_mask = ((lax.broadcasted_iota(i32, (PAIR, PAIR), 0) < HEAD_SIZE)
               == (lax.broadcasted_iota(i32, (PAIR, PAIR), 1) < HEAD_SIZE))
    masks = (head0, strict, incl, eye, bd_mask)
    tril = tril_ref[...]
    ones_bd = ones_ref[...]
    vec = vec_ref[...]
    ln_w, ln_b, r_k = vec[0], vec[1], vec[2]
    rows = min(tc, n)
    nchunks = max(tc // n, 1)

    def load(ref, bl, ci):
        x = ref[bl, ci * n:ci * n + rows, :].astype(f32)
        if rows < n:
            x = jnp.concatenate([x, jnp.zeros((n - rows, x.shape[1]), f32)], axis=0)
        return x

    for bl in range(bs):
        for ci in range(nchunks):
            r = load(r_ref, bl, ci)
            k = load(k_ref, bl, ci)
            v = load(v_ref, bl, ci)
            kk = load(kk_ref, bl, ci)
            beta = load(b_ref, bl, ci)
            g = load(g_ref, bl, ci)
            lw = load(lw_ref, bl, ci)
            hi = lw.astype(bf16)
            rem = lw - hi.astype(f32)
            mid = rem.astype(bf16)
            lo = (rem - mid.astype(f32)).astype(bf16)
            c = _dot(tril, hi) + _dot(tril, mid) + _dot(tril, lo)
            outs = []
            for pi in range(npair):
                sl = slice(pi * PAIR, (pi + 1) * PAIR)
                o, s_new = _wkv_chunk(r[:, sl], k[:, sl], v[:, sl], kk[:, sl], beta[:, sl], lw[:, sl],
                                      c[:, sl], s_scr[bl, pi], masks)
                s_scr[bl, pi] = s_new
                outs.append(o)
            o = jnp.concatenate(outs, axis=1) if npair > 1 else outs[0]
            mean = _seg_sum(o, ones_bd) * (1.0 / HEAD_SIZE)
            dev = o - mean
            var = _seg_sum(dev * dev, ones_bd) * (1.0 / HEAD_SIZE)
            y = dev * lax.rsqrt(var + GN_EPS) * ln_w + ln_b
            y = y + _seg_sum(r * k * r_k, ones_bd) * v
            z = (y * g).astype(bf16)
            z_ref[bl, ci * n:ci * n + rows, :] = z[:rows]

    @pl.when(t == pl.num_programs(2) - 1)
    def _():
        so_ref[...] = s_scr[...]


def _wkv(r, k, v, kk, beta, g, lw, state_bd, vec, tril, ones_bd, *, bs, tc, npair):
    bsz, seq, d = r.shape
    wl = npair * PAIR
    body = functools.partial(_wkv_body, bs=bs, tc=tc, npair=npair)
    tok = pl.BlockSpec((bs, tc, wl), lambda b, p, t: (b, t, p))
    st = pl.BlockSpec((bs, npair, PAIR, PAIR), lambda b, p, t: (b, p, 0, 0))
    return pl.pallas_call(
        body,
        grid=(bsz // bs, d // wl, seq // tc),
        in_specs=[tok] * 7 + [
            st,
            pl.BlockSpec((8, wl), lambda b, p, t: (0, p)),
            pl.BlockSpec((WKV_CHUNK, WKV_CHUNK), lambda b, p, t: (0, 0)),
            pl.BlockSpec((wl, wl), lambda b, p, t: (0, 0)),
        ],
        out_specs=[tok, st],
        out_shape=[jax.ShapeDtypeStruct((bsz, seq, d), bf16),
                   jax.ShapeDtypeStruct(state_bd.shape, f32)],
        scratch_shapes=[pltpu.VMEM((bs, npair, PAIR, PAIR), f32)],
        compiler_params=_cparams(("arbitrary", "arbitrary", "arbitrary")),
        name="wkv",
    )(r, k, v, kk, beta, g, lw, state_bd, vec, tril, ones_bd)


def _block_ones(n):
    idx = jnp.arange(n) // HEAD_SIZE
    return (idx[:, None] == idx[None, :]).astype(bf16)


def _state_to_pairs(s):
    b = s.shape[0]
    s5 = s.reshape(b, N_PAIRS, 2, HEAD_SIZE, HEAD_SIZE)
    eye = jnp.eye(2, dtype=s.dtype)
    bd = s5[:, :, :, :, None, :] * eye[None, None, :, None, :, None]
    return bd.reshape(b, N_PAIRS, PAIR, PAIR)


def _pairs_to_state(bd):
    b = bd.shape[0]
    s6 = bd.reshape(b, N_PAIRS, 2, HEAD_SIZE, 2, HEAD_SIZE)
    s = jnp.stack([s6[:, :, 0, :, 0, :], s6[:, :, 1, :, 1, :]], axis=2)
    return s.reshape(b, N_HEADS, HEAD_SIZE, HEAD_SIZE)


_PAIR_TABLE = ((0, 1), (0, 2), (0, 3), (1, 2), (1, 3), (2, 3))


def _moe_plan(counts, n_tiles, tm):
    cnt = counts[:N_CLASSES].astype(i32)
    tiles_per = (cnt + tm - 1) // tm
    tile_end = jnp.cumsum(tiles_per)
    tile_start = tile_end - tiles_per
    offsets = tile_start * tm
    tidx = jnp.arange(n_tiles, dtype=i32)
    cls_of_tile = jnp.sum((tidx[:, None] >= tile_end[None, :]).astype(i32), axis=1)
    valid = (cls_of_tile < N_CLASSES).astype(i32)
    last_cls = jnp.max(jnp.where(tiles_per > 0, jnp.arange(N_CLASSES, dtype=i32), 0))
    cls_c = jnp.where(valid > 0, cls_of_tile, last_cls)
    grp = cls_c // 6
    pr = cls_c % 6
    lo_tab = jnp.array([p[0] for p in _PAIR_TABLE], i32)
    hi_tab = jnp.array([p[1] for p in _PAIR_TABLE], i32)
    ea = grp * EXPERTS_PER_GROUP + lo_tab[pr]
    eb = grp * EXPERTS_PER_GROUP + hi_tab[pr]
    return offsets, ea, eb, valid


def _positions(info, offsets):
    cls = info[:, 0, :].reshape(-1)
    rank = info[:, 1, :].reshape(-1)
    return offsets[cls] + rank


def _router_weights(wg, bg, we, be):
    w = jnp.concatenate([wg, we], axis=1).T
    w = jnp.pad(w, ((0, ROUTER_ROWS - w.shape[0]), (0, 0)))
    hi = w.astype(bf16)
    lo = (w - hi.astype(f32)).astype(bf16)
    bias = jnp.pad(jnp.concatenate([bg, be]), (0, ROUTER_ROWS - N_GROUPS - N_EXPERTS))
    return jnp.concatenate([hi, lo], axis=0), bias.reshape(ROUTER_ROWS, 1)


def _tiles(bsz, seq, rows):
    tt = min(seq, rows)
    bs = max(rows // tt, 1)
    assert bsz % bs == 0 and seq % tt == 0
    return bs, tt


def kernel(x_prompt, x_sample, c_prompt, c_sample, state_conv, state_shift, state_wkv, ada_w, ada_b, norm_g, final_g, cv_in, cv_w, cv_out, rw_mu, rw_r, rw_k, rw_v, rw_o, rw_w0, rw_w1, rw_w2, rw_a0, rw_a1, rw_a2, rw_g1, rw_g2, rw_kk, rw_ka, rw_rk, rw_lnw, rw_lnb, moe_wg, moe_bg, moe_we, moe_be, moe_w1, moe_w3, moe_w2):
    d = D_MODEL
    xs = [x_prompt, x_sample]
    bszs = [x.shape[0] for x in xs]
    seqs = [x.shape[1] for x in xs]
    ntok = [b * s for b, s in zip(bszs, seqs)]
    n_total = sum(ntok)

    c_all = jnp.concatenate([c_prompt, c_sample], axis=0)
    rows_c = c_all.shape[0]
    rows_pad = -(-rows_c // 8) * 8
    mod = _ada(jnp.pad(c_all, ((0, rows_pad - rows_c), (0, 0))), ada_w, ada_b)

    def mods(layer, trunk):
        lo = 0 if trunk == 0 else bszs[0]
        m = mod[layer, lo:lo + bszs[trunk]]
        return [m[:, i * d:(i + 1) * d].reshape(bszs[trunk], 1, d) for i in range(N_ADA)]

    conv_states = [jnp.zeros((1, bszs[0], CONV_WIDTH - 1, d), f32), state_conv]
    shift_states = [jnp.zeros((1, bszs[0], d), f32), state_shift]
    wkv_states = [jnp.zeros((1, bszs[0], N_HEADS, HEAD_SIZE, HEAD_SIZE), f32), state_wkv]

    tri_cache = {}

    def tri(n):
        if n not in tri_cache:
            tri_cache[n] = (jnp.arange(n)[:, None] < jnp.arange(n)[None, :]).astype(bf16)
        return tri_cache[n]

    n_tiles = n_total // MOE_TILE + N_CLASSES
    p_rows = n_tiles * MOE_TILE

    def moe_layer(layer, rows_list, info_list, counts):
        offsets, ea, eb, valid = _moe_plan(counts[:, 0], n_tiles, MOE_TILE)
        pos_list = [_positions(info, offsets) for info in info_list]
        sorted_rows = jnp.zeros((p_rows, ROW_WORDS), u32)
        for rows, pos in zip(rows_list, pos_list):
            flat = rows.reshape(-1, ROW_WORDS)
            sorted_rows = _dispatch(flat, pos.reshape(-1, 1, MOE_TILE), sorted_rows, tm=MOE_TILE)
        ys = _moe(sorted_rows, ea, eb, valid, moe_w1[layer].astype(bf16), moe_w3[layer].astype(bf16),
                  moe_w2[layer].astype(bf16), tm=MOE_TILE)
        return ys, pos_list

    def out_router_pair(layer, acts, w_bf, xcur, tails=(None, None)):
        wrt, rbias = _router_weights(moe_wg[layer], moe_bg[layer], moe_we[layer], moe_be[layer])
        counts = jnp.zeros((CLASS_ROWS, 128), f32)
        x_new, rows_list, info_list = [], [], []
        for trunk in range(2):
            _, _, gt_m, sh_f, sc_f, _ = mods(layer, trunk)
            bs, tt = _tiles(bszs[trunk], seqs[trunk], 256)
            xo, rows, info, counts = _out_router(
                acts[trunk], w_bf, xcur[trunk], gt_m, norm_g[layer, 1].reshape(1, d), sh_f, sc_f,
                wrt, rbias, counts, tri(bs * tt), bs=bs, tt=tt, tail=tails[trunk])
            x_new.append(xo)
            rows_list.append(rows)
            info_list.append(info)
        return x_new, rows_list, info_list, counts

    def combine_pair(layer, ys, pos_list, xcur, final):
        out = []
        for trunk in range(2):
            gt_f = mods(layer, trunk)[5]
            bs, tt = _tiles(bszs[trunk], seqs[trunk], 256)
            out.append(_combine(ys, pos_list[trunk].reshape(-1, 1, bs * tt), xcur[trunk], gt_f,
                                final_g.reshape(1, d), bs=bs, tt=tt, final_norm=final))
        return out

    w_in_hi, w_in_lo = _split2(cv_in[0])
    w_out_hi, w_out_lo = _split2(cv_out[0])
    gain0 = norm_g[0, 0].reshape(1, d)
    z0, conv_out = [], []
    for trunk in range(2):
        sh_m, sc_m = mods(0, trunk)[:2]
        bs, tt = _tiles(bszs[trunk], seqs[trunk], 512)
        st8 = jnp.pad(conv_states[trunk][0], ((0, 0), (CARRY_ROWS - (CONV_WIDTH - 1), 0), (0, 0)))
        z, so = _conv_in(xs[trunk], sh_m, sc_m, gain0, w_in_hi, cv_w[0], st8, bs=bs, tt=tt)
        z0.append(z)
        conv_out.append(so[:, -1, CARRY_ROWS - (CONV_WIDTH - 1):, :][None])
    assert min(seqs) >= TAIL_WINDOW
    x_tail = jnp.concatenate([x[:, -TAIL_WINDOW:, :] for x in xs], axis=0)
    n_seq = x_tail.shape[0]
    mod_tail = [jnp.concatenate([mods(0, trunk)[i] for trunk in range(2)], axis=0) for i in range(3)]
    z_tail, _ = _conv_in(x_tail, mod_tail[0], mod_tail[1], gain0, w_in_hi, cv_w[0],
                         jnp.zeros((n_seq, CARRY_ROWS, d), f32), bs=n_seq, tt=TAIL_WINDOW, w_in_lo=w_in_lo)
    x1_tail = _tail_out(z_tail, x_tail, mod_tail[2], w_out_hi, w_out_lo)[:, TAIL_WINDOW - TAIL_ROWS:, :]
    tails = (x1_tail[:bszs[0]], x1_tail[bszs[0]:])
    x1, rows0, info0, counts0 = out_router_pair(0, z0, w_out_hi, xs, tails)
    ys0, pos0 = moe_layer(0, rows0, info0, counts0)
    x2 = combine_pair(0, ys0, pos0, x1, False)

    pad_l = ((0, 0), (0, LORA_PAD - rw_w1.shape[2]))
    pad_r = ((0, LORA_PAD - rw_w2.shape[1]), (0, 0))
    w1p = jnp.pad(rw_w1[0], pad_l).astype(bf16)
    a1p = jnp.pad(rw_a1[0], pad_l).astype(bf16)
    w2p = jnp.pad(rw_w2[0], pad_r).astype(bf16)
    a2p = jnp.pad(rw_a2[0], pad_r).astype(bf16)
    vec_p = jnp.pad(jnp.stack([rw_w0[0], rw_a0[0], rw_kk[0], rw_ka[0]]), ((0, 4), (0, 0)))
    vec_w = jnp.pad(jnp.stack([rw_lnw[0], rw_lnb[0], rw_rk[0].reshape(d)]), ((0, 5), (0, 0)))
    wr_bf, wk_bf, wv_bf = rw_r[0].astype(bf16), rw_k[0].astype(bf16), rw_v[0].astype(bf16)
    g1_bf, g2_bf = rw_g1[0].astype(bf16), rw_g2[0].astype(bf16)
    proj_tn = 256
    npair = 4
    tril = (jnp.arange(WKV_CHUNK)[:, None] >= jnp.arange(WKV_CHUNK)[None, :]).astype(bf16)
    z1, shift_out, wkv_out = [], [], []
    for trunk in range(2):
        sh_m, sc_m = mods(1, trunk)[:2]
        bs, tt = _tiles(bszs[trunk], seqs[trunk], 512)
        r, k, v, kk, beta, g, lw, sho = _rwkv_proj(
            x2[trunk], sh_m, sc_m, norm_g[1, 0].reshape(1, d), rw_mu[0],
            shift_states[trunk][0].reshape(bszs[trunk], 1, d), wr_bf, wk_bf, wv_bf, w1p, a1p, g1_bf,
            w2p, a2p, g2_bf, vec_p, _block_ones(proj_tn), bs=bs, tt=tt, tn=proj_tn)
        tc = min(seqs[trunk], 2 * WKV_CHUNK)
        z, s_bd = _wkv(r, k, v, kk, beta, g, lw, _state_to_pairs(wkv_states[trunk][0]), vec_w, tril,
                       _block_ones(npair * PAIR), bs=1, tc=tc, npair=npair)
        z1.append(z)
        shift_out.append(sho.reshape(1, bszs[trunk], d))
        wkv_out.append(_pairs_to_state(s_bd)[None])
    x3, rows1, info1, counts1 = out_router_pair(1, z1, rw_o[0].astype(bf16), x2)
    ys1, pos1 = moe_layer(1, rows1, info1, counts1)
    y = combine_pair(1, ys1, pos1, x3, True)

    return (y[0], y[1], conv_out[0], shift_out[0], wkv_out[0], conv_out[1], shift_out[1], wkv_out[1])
```

```python
import functools

import jax
import jax.numpy as jnp
from jax import lax
from jax.experimental import pallas as pl
from jax.experimental.pallas import tpu as pltpu

f32 = jnp.float32
bf16 = jnp.bfloat16
i32 = jnp.int32
u32 = jnp.uint32

D_MODEL = 2048
LANES = 128
SUBLANES = 8
HEAD_SIZE = 64
N_HEADS = D_MODEL // HEAD_SIZE
PAIR = 2 * HEAD_SIZE
N_PAIRS = D_MODEL // PAIR
CONV_WIDTH = 3
N_GROUPS = 4
EXPERTS_PER_GROUP = 4
N_EXPERTS = N_GROUPS * EXPERTS_PER_GROUP
D_EXPERT = D_MODEL // 4
N_ADA = 6
N_SHIFT_MIX = 6
RMS_EPS = 1e-6
GN_EPS = 64e-5
LORA_PAD = 128
N_CLASSES = N_GROUPS * 6
CLASS_ROWS = 32
ROUTER_ROWS = 128
HALF = D_MODEL // 2
ROW_WORDS = HALF + 128
WKV_CHUNK = 64
CARRY_ROWS = 8
TAIL_ROWS = 8
TAIL_WINDOW = 2 * TAIL_ROWS

V7X_VMEM_LIMIT = 56 * 1024 * 1024
MOE_TILE = 512
DMA_UNROLL = 8
ROUTER_TILE = 512
COMBINE_TILE = 512
PROJ_TILE = 512


def _cparams(sem, vmem=V7X_VMEM_LIMIT):
    return pltpu.CompilerParams(dimension_semantics=sem, vmem_limit_bytes=vmem)


def _dot(a, b):
    return jnp.dot(a, b, preferred_element_type=f32)


def _dot_nt(a, b):
    return lax.dot_general(a, b, (((1,), (1,)), ((), ())), preferred_element_type=f32)


def _dot_tn(a, b):
    return lax.dot_general(a, b, (((0,), (0,)), ((), ())), preferred_element_type=f32)


def _split2(x):
    hi = pltpu.bitcast(pltpu.bitcast(x, u32) & jnp.uint32(0xFFFF0000), f32)
    return hi.astype(bf16), (x - hi).astype(bf16)


def _split2_outside(x):
    hi = lax.bitcast_convert_type(lax.bitcast_convert_type(x, u32) & jnp.uint32(0xFFFF0000), f32)
    return hi.astype(bf16), (x - hi).astype(bf16)


def _sigmoid(x):
    return 1.0 / (1.0 + jnp.exp(-x))


def _pack_rows(x):
    hi = pltpu.bitcast(x[:, :HALF].astype(bf16).astype(f32), u32)
    lo = pltpu.bitcast(x[:, HALF:].astype(bf16).astype(f32), u32)
    return (hi & jnp.uint32(0xFFFF0000)) | (lo >> 16)


def _unpack_rows(p):
    left = pltpu.bitcast(p & jnp.uint32(0xFFFF0000), f32)
    right = pltpu.bitcast(p << 16, f32)
    return left, right


def _dot3(a_hi, a_lo, w_hi, w_lo):
    return _dot(a_hi, w_hi) + _dot(a_lo, w_hi) + _dot(a_hi, w_lo)


def _ada_body(c_ref, w_ref, b_ref, o_ref):
    c = c_ref[...]
    s_hi, s_lo = _split2(c * _sigmoid(c))
    w_hi, w_lo = _split2(w_ref[0])
    o_ref[0] = _dot3(s_hi, s_lo, w_hi, w_lo) + b_ref[0]


def _ada(c_all, ada_w, ada_b, tn=1024):
    depth, d, n = ada_w.shape
    rows = c_all.shape[0]
    return pl.pallas_call(
        _ada_body,
        grid=(depth, n // tn),
        in_specs=[
            pl.BlockSpec((rows, d), lambda l, j: (0, 0)),
            pl.BlockSpec((1, d, tn), lambda l, j: (l, 0, j)),
            pl.BlockSpec((1, 1, tn), lambda l, j: (l, 0, j)),
        ],
        out_specs=pl.BlockSpec((1, rows, tn), lambda l, j: (l, 0, j)),
        out_shape=jax.ShapeDtypeStruct((depth, rows, n), f32),
        compiler_params=_cparams(("arbitrary", "arbitrary")),
        name="ada",
    )(c_all, ada_w, ada_b.reshape(depth, 1, n))


def _inv_rms(x2):
    sq = x2 * x2
    part = sq[:, 0:LANES]
    for i in range(1, D_MODEL // LANES):
        part = part + sq[:, i * LANES:(i + 1) * LANES]
    hi, lo = _split2(part)
    ones = jnp.ones((LANES, LANES), bf16)
    ms = (_dot(hi, ones) + _dot(lo, ones)) * (1.0 / D_MODEL)
    rs = lax.rsqrt(ms + RMS_EPS)
    return jnp.concatenate([rs] * (D_MODEL // LANES), axis=1)


def _per_seq(x, fn, *vecs):
    bs, tt, n = x.shape
    x4 = x.reshape(bs, tt // SUBLANES, SUBLANES, n)
    return fn(x4, *(v[:, None] for v in vecs)).reshape(bs, tt, n)


def _gated_add(x, gate, y):
    return _per_seq(y, lambda y4, g4: g4 * y4, gate) + x


def _norm_mod(x, gain, sh, sc):
    bs, tt, d = x.shape
    x2 = x.reshape(bs * tt, d)
    y = (x2 * _inv_rms(x2) * gain).reshape(bs, tt, d)
    return _per_seq(y, lambda y4, sh4, sc4: y4 * (1.0 + sc4) + sh4, sh, sc)


def _conv_in_body(x_ref, sh_ref, sc_ref, g_ref, cw_ref, st_ref, *rest, bs, tt, tn, precise):
    nw = 6 if precise else 3
    w_refs = rest[:nw]
    z_ref, so_ref, h_scr, carry_scr, uext_scr = rest[nw:]
    t = pl.program_id(1)
    j = pl.program_id(2)

    @pl.when(j == 0)
    def _():
        h = _norm_mod(x_ref[...], g_ref[...], sh_ref[...], sc_ref[...]).reshape(bs * tt, D_MODEL)
        if precise:
            h_scr[0], h_scr[1] = _split2(h)
        else:
            h_scr[0] = h.astype(bf16)

    @pl.when(t == 0)
    def _():
        carry_scr[j] = st_ref[...]

    if precise:
        b_gate, c_gate, xv = (_dot3(h_scr[0], h_scr[1], w_refs[i][...], w_refs[i + 3][...]) for i in range(3))
    else:
        b_gate, c_gate, xv = (_dot(h_scr[0], w_refs[i][...]) for i in range(3))
    u = (c_gate * xv).reshape(bs, tt, tn)
    uext_scr[:, 0:CARRY_ROWS, :] = carry_scr[j]
    uext_scr[:, CARRY_ROWS:, :] = u
    cw = cw_ref[...]
    tap = lambda i: cw[i * SUBLANES:(i + 1) * SUBLANES]
    tiles = lambda x: x.reshape(bs, tt // SUBLANES, SUBLANES, tn)
    conv = (tiles(uext_scr[:, CARRY_ROWS - 2:CARRY_ROWS - 2 + tt, :]) * tap(0)
            + tiles(uext_scr[:, CARRY_ROWS - 1:CARRY_ROWS - 1 + tt, :]) * tap(1)
            + tiles(u) * tap(2))
    z_ref[...] = (b_gate.reshape(bs, tt, tn) * conv.reshape(bs, tt, tn)).astype(z_ref.dtype)
    last = uext_scr[:, tt:tt + CARRY_ROWS, :]
    carry_scr[j] = last
    so_ref[:, 0] = last


def _conv_in(x, sh, sc, gain, w_in_hi, conv_w, state8, *, bs, tt, tn=512, w_in_lo=None):
    bsz, seq, d = x.shape
    nj = d // tn
    precise = w_in_lo is not None
    body = functools.partial(_conv_in_body, bs=bs, tt=tt, tn=tn, precise=precise)
    w_specs = [pl.BlockSpec((d, tn), lambda b, t, j, k=k: (0, j + k * nj)) for k in range(3)]
    weights = [w_in_hi] * 3 + ([w_in_lo] * 3 if precise else [])
    return pl.pallas_call(
        body,
        grid=(bsz // bs, seq // tt, nj),
        in_specs=[
            pl.BlockSpec((bs, tt, d), lambda b, t, j: (b, t, 0)),
            pl.BlockSpec((bs, SUBLANES, d), lambda b, t, j: (b, 0, 0)),
            pl.BlockSpec((bs, SUBLANES, d), lambda b, t, j: (b, 0, 0)),
            pl.BlockSpec((1, d), lambda b, t, j: (0, 0)),
            pl.BlockSpec((CONV_WIDTH * SUBLANES, tn), lambda b, t, j: (0, j)),
            pl.BlockSpec((bs, CARRY_ROWS, tn), lambda b, t, j: (b, 0, j)),
        ] + w_specs * (2 if precise else 1),
        out_specs=[
            pl.BlockSpec((bs, tt, tn), lambda b, t, j: (b, t, j)),
            pl.BlockSpec((bs, 1, CARRY_ROWS, tn), lambda b, t, j: (b, t, 0, j)),
        ],
        out_shape=[
            jax.ShapeDtypeStruct((bsz, seq, d), f32 if precise else bf16),
            jax.ShapeDtypeStruct((bsz, seq // tt, CARRY_ROWS, d), f32),
        ],
        scratch_shapes=[
            pltpu.VMEM((2 if precise else 1, bs * tt, d), bf16),
            pltpu.VMEM((nj, bs, CARRY_ROWS, tn), f32),
            pltpu.VMEM((bs, tt + CARRY_ROWS, tn), f32),
        ],
        compiler_params=_cparams(("arbitrary", "arbitrary", "arbitrary")),
        name="conv_in_precise" if precise else "conv_in",
    )(x, sh, sc, gain, conv_w, state8, *weights)


def _tail_out_body(z_ref, x_ref, gt_ref, wh_ref, wl_ref, o_ref, *, bs, tt):
    z_hi, z_lo = _split2(z_ref[...].reshape(bs * tt, D_MODEL))
    y = _dot3(z_hi, z_lo, wh_ref[...], wl_ref[...])
    tn = y.shape[1]
    o_ref[...] = _gated_add(x_ref[...], gt_ref[...], y.reshape(bs, tt, tn))


def _tail_out(z, x, gt, w_hi, w_lo, *, tn=512):
    bsz, tt, d = x.shape
    body = functools.partial(_tail_out_body, bs=bsz, tt=tt)
    return pl.pallas_call(
        body,
        grid=(d // tn,),
        in_specs=[
            pl.BlockSpec((bsz, tt, d), lambda j: (0, 0, 0)),
            pl.BlockSpec((bsz, tt, tn), lambda j: (0, 0, j)),
            pl.BlockSpec((bsz, SUBLANES, tn), lambda j: (0, 0, j)),
            pl.BlockSpec((d, tn), lambda j: (0, j)),
            pl.BlockSpec((d, tn), lambda j: (0, j)),
        ],
        out_specs=pl.BlockSpec((bsz, tt, tn), lambda j: (0, 0, j)),
        out_shape=jax.ShapeDtypeStruct((bsz, tt, d), f32),
        compiler_params=_cparams(("arbitrary",)),
        name="tail_out",
    )(z, x, gt, w_hi, w_lo)


def _route_rows(logit):
    lg = [logit[g:g + 1, :] for g in range(N_GROUPS)]
    le = [logit[N_GROUPS + e:N_GROUPS + e + 1, :] for e in range(N_EXPERTS)]
    gmax = jnp.maximum(jnp.maximum(lg[0], lg[1]), jnp.maximum(lg[2], lg[3]))
    gidx = jnp.where(lg[0] == gmax, 0, jnp.where(lg[1] == gmax, 1, jnp.where(lg[2] == gmax, 2, 3)))
    denom = (jnp.exp(lg[0] - gmax) + jnp.exp(lg[1] - gmax)
             + jnp.exp(lg[2] - gmax) + jnp.exp(lg[3] - gmax))
    p_grp = 1.0 / denom
    leg = [jnp.where(gidx == 0, le[i],
                     jnp.where(gidx == 1, le[EXPERTS_PER_GROUP + i],
                               jnp.where(gidx == 2, le[2 * EXPERTS_PER_GROUP + i],
                                         le[3 * EXPERTS_PER_GROUP + i])))
           for i in range(EXPERTS_PER_GROUP)]
    v1 = jnp.maximum(jnp.maximum(leg[0], leg[1]), jnp.maximum(leg[2], leg[3]))
    i1 = jnp.where(leg[0] == v1, 0, jnp.where(leg[1] == v1, 1, jnp.where(leg[2] == v1, 2, 3)))
    neg = jnp.float32(-jnp.inf)
    rest = [jnp.where(i1 == i, neg, leg[i]) for i in range(EXPERTS_PER_GROUP)]
    v2 = jnp.maximum(jnp.maximum(rest[0], rest[1]), jnp.maximum(rest[2], rest[3]))
    i2 = jnp.where((rest[0] == v2) & (i1 != 0), 0,
                   jnp.where((rest[1] == v2) & (i1 != 1), 1,
                             jnp.where((rest[2] == v2) & (i1 != 2), 2, 3)))
    s = jnp.exp(v2 - v1)
    w_first = p_grp / (1.0 + s)
    w_second = p_grp * s / (1.0 + s)
    i_lo = jnp.minimum(i1, i2)
    i_hi = jnp.maximum(i1, i2)
    pair_base = jnp.where(i_lo == 0, 0, jnp.where(i_lo == 1, 3, 5))
    cls = gidx * 6 + pair_base + (i_hi - i_lo - 1)
    g_lo = jnp.where(i1 < i2, w_first, w_second)
    g_hi = jnp.where(i1 < i2, w_second, w_first)
    return cls, g_lo, g_hi


def _out_router_body(a_ref, w_ref, x_ref, gt_ref, g_ref, sh_ref, sc_ref, wrt_ref, rb_ref,
                     cin_ref, tri_ref, *rest, bs, tt, with_tail):
    if with_tail:
        tail_ref, xo_ref, rows_ref, info_ref, cout_ref, cnt_scr = rest
    else:
        xo_ref, rows_ref, info_ref, cout_ref, cnt_scr = rest
    tm = bs * tt
    first = (pl.program_id(0) == 0) & (pl.program_id(1) == 0)

    @pl.when(first)
    def _():
        cnt_scr[...] = cin_ref[...]

    y = _dot(a_ref[...].reshape(tm, D_MODEL), w_ref[...])
    xo_ref[...] = _gated_add(x_ref[...], gt_ref[...], y.reshape(bs, tt, D_MODEL))
    if with_tail:
        @pl.when(pl.program_id(1) == pl.num_programs(1) - 1)
        def _():
            xo_ref[:, tt - TAIL_ROWS:, :] = tail_ref[...]
    xn = xo_ref[...]
    h = _norm_mod(xn, g_ref[...], sh_ref[...], sc_ref[...]).reshape(tm, D_MODEL)
    h_hi, h_lo = _split2(h)
    wrt = wrt_ref[...]
    p_hi = _dot_nt(wrt, h_hi)
    p_lo = _dot_nt(wrt[:ROUTER_ROWS], h_lo)
    logit = p_hi[:ROUTER_ROWS] + p_hi[ROUTER_ROWS:] + p_lo + rb_ref[...]
    cls, g_lo, g_hi = _route_rows(logit)

    crow = lax.broadcasted_iota(i32, (CLASS_ROWS, tm), 0)
    onehot = (crow == cls).astype(f32)
    before = _dot(onehot.astype(bf16), tri_ref[...])
    base = cnt_scr[:, 0:1]
    rank = jnp.sum(onehot * (before + base), axis=0, keepdims=True).astype(i32)
    cnt_new = cnt_scr[...] + jnp.sum(onehot, axis=1, keepdims=True)
    cnt_scr[...] = cnt_new
    cout_ref[...] = cnt_new

    irow = lax.broadcasted_iota(i32, (8, tm), 0)
    info_ref[0] = jnp.where(irow == 0, cls, jnp.where(irow == 1, rank, 0))

    grow = lax.broadcasted_iota(i32, (ROUTER_ROWS, tm), 0)
    gates_t = jnp.where(grow == 0, g_lo, jnp.where(grow == 1, g_hi, 0.0))
    gates = pltpu.bitcast(gates_t.T, u32)
    rows_ref[:, :HALF] = _pack_rows(h)
    rows_ref[:, HALF:] = gates


def _out_router(a, w_bf, x, gt, gain, sh, sc, wrt, rbias, cnt_in, tri, *, bs, tt, tail=None):
    bsz, seq, d = x.shape
    tm = bs * tt
    nt = seq // tt
    ntiles = (bsz // bs) * nt
    with_tail = tail is not None
    body = functools.partial(_out_router_body, bs=bs, tt=tt, with_tail=with_tail)
    tail_specs = [pl.BlockSpec((bs, TAIL_ROWS, d), lambda b, t: (b, 0, 0))] if with_tail else []
    tail_args = [tail] if with_tail else []
    return pl.pallas_call(
        body,
        grid=(bsz // bs, nt),
        in_specs=[
            pl.BlockSpec((bs, tt, d), lambda b, t: (b, t, 0)),
            pl.BlockSpec((d, d), lambda b, t: (0, 0), pipeline_mode=pl.Buffered(1)),
            pl.BlockSpec((bs, tt, d), lambda b, t: (b, t, 0)),
            pl.BlockSpec((bs, SUBLANES, d), lambda b, t: (b, 0, 0)),
            pl.BlockSpec((1, d), lambda b, t: (0, 0)),
            pl.BlockSpec((bs, SUBLANES, d), lambda b, t: (b, 0, 0)),
            pl.BlockSpec((bs, SUBLANES, d), lambda b, t: (b, 0, 0)),
            pl.BlockSpec((2 * ROUTER_ROWS, d), lambda b, t: (0, 0)),
            pl.BlockSpec((ROUTER_ROWS, 1), lambda b, t: (0, 0)),
            pl.BlockSpec((CLASS_ROWS, 128), lambda b, t: (0, 0)),
            pl.BlockSpec((tm, tm), lambda b, t: (0, 0)),
        ] + tail_specs,
        out_specs=[
            pl.BlockSpec((bs, tt, d), lambda b, t: (b, t, 0)),
            pl.BlockSpec((tm, ROW_WORDS), lambda b, t: (b * nt + t, 0)),
            pl.BlockSpec((1, 8, tm), lambda b, t: (b * nt + t, 0, 0)),
            pl.BlockSpec((CLASS_ROWS, 128), lambda b, t: (0, 0)),
        ],
        out_shape=[
            jax.ShapeDtypeStruct((bsz, seq, d), f32),
            jax.ShapeDtypeStruct((bsz * seq, ROW_WORDS), u32),
            jax.ShapeDtypeStruct((ntiles, 8, tm), i32),
            jax.ShapeDtypeStruct((CLASS_ROWS, 128), f32),
        ],
        scratch_shapes=[pltpu.VMEM((CLASS_ROWS, 128), f32)],
        compiler_params=_cparams(("arbitrary", "arbitrary")),
        name="out_router",
    )(a, w_bf, x, gt, gain, sh, sc, wrt, rbias, cnt_in, tri, *tail_args)


def _row_copy_scatter(src_ref, dst_ref, pos_ref, sem, r):
    return pltpu.make_async_copy(src_ref.at[pl.ds(r, 1), :], dst_ref.at[pl.ds(pos_ref[0, 0, r], 1), :], sem)


def _dispatch_body(pos_ref, rows_ref, dst_in_ref, dst_ref, sem, *, tm):
    del dst_in_ref

    def start(r, carry):
        _row_copy_scatter(rows_ref, dst_ref, pos_ref, sem, r).start()
        return carry

    lax.fori_loop(0, tm, start, 0, unroll=DMA_UNROLL)
    def wait(r, carry):
        _row_copy_scatter(rows_ref, dst_ref, pos_ref, sem, r).wait()
        return carry

    lax.fori_loop(0, tm, wait, 0, unroll=DMA_UNROLL)


def _dispatch(rows, pos3, sorted_rows, *, tm):
    n = rows.shape[0]
    body = functools.partial(_dispatch_body, tm=tm)
    return pl.pallas_call(
        body,
        grid=(n // tm,),
        in_specs=[
            pl.BlockSpec((1, 1, tm), lambda i: (i, 0, 0), memory_space=pltpu.SMEM),
            pl.BlockSpec((tm, ROW_WORDS), lambda i: (i, 0)),
            pl.BlockSpec(memory_space=pl.ANY),
        ],
        out_specs=pl.BlockSpec(memory_space=pl.ANY),
        out_shape=jax.ShapeDtypeStruct(sorted_rows.shape, sorted_rows.dtype),
        scratch_shapes=[pltpu.SemaphoreType.DMA(())],
        input_output_aliases={2: 0},
        compiler_params=_cparams(("arbitrary",)),
        name="moe_dispatch",
    )(pos3, rows, sorted_rows)


def _moe_body(ea_ref, eb_ref, valid_ref, xs_ref, w1a_ref, w3a_ref, w2a_ref, w1b_ref, w3b_ref, w2b_ref,
              ys_ref):
    del ea_ref, eb_ref
    i = pl.program_id(0)

    @pl.when(valid_ref[i] > 0)
    def _():
        left, right = _unpack_rows(xs_ref[:, :HALF])
        x = jnp.concatenate([left.astype(bf16), right.astype(bf16)], axis=1)
        gates = pltpu.bitcast(xs_ref[:, HALF:], f32)

        def expert(w1_ref, w3_ref, w2_ref, gate):
            h1 = _dot(x, w1_ref[0])
            h3 = _dot(x, w3_ref[0])
            hid = (h1 * _sigmoid(h1)) * h3 * gate
            return _dot(hid.astype(bf16), w2_ref[0])

        y = expert(w1a_ref, w3a_ref, w2a_ref, gates[:, 0:1]) + expert(w1b_ref, w3b_ref, w2b_ref, gates[:, 1:2])
        ys_ref[...] = _pack_rows(y)

    @pl.when(valid_ref[i] == 0)
    def _():
        ys_ref[...] = jnp.zeros(ys_ref.shape, u32)


def _moe(sorted_rows, tile_ea, tile_eb, tile_valid, w1_bf, w3_bf, w2_bf, *, tm):
    p = sorted_rows.shape[0]
    d, f = D_MODEL, D_EXPERT

    def wa(i, ea, eb, valid):
        return (ea[i], 0, 0)

    def wb(i, ea, eb, valid):
        return (eb[i], 0, 0)

    grid_spec = pltpu.PrefetchScalarGridSpec(
        num_scalar_prefetch=3,
        grid=(p // tm,),
        in_specs=[
            pl.BlockSpec((tm, ROW_WORDS), lambda i, ea, eb, valid: (i, 0)),
            pl.BlockSpec((1, d, f), wa), pl.BlockSpec((1, d, f), wa), pl.BlockSpec((1, f, d), wa),
            pl.BlockSpec((1, d, f), wb), pl.BlockSpec((1, d, f), wb), pl.BlockSpec((1, f, d), wb),
        ],
        out_specs=pl.BlockSpec((tm, HALF), lambda i, ea, eb, valid: (i, 0)),
    )
    return pl.pallas_call(
        _moe_body,
        grid_spec=grid_spec,
        out_shape=jax.ShapeDtypeStruct((p, HALF), u32),
        compiler_params=_cparams(("arbitrary",)),
        name="moe_experts",
    )(tile_ea, tile_eb, tile_valid, sorted_rows, w1_bf, w3_bf, w2_bf, w1_bf, w3_bf, w2_bf)


def _row_copy_gather(src_ref, dst_ref, pos_ref, sem, r):
    return pltpu.make_async_copy(src_ref.at[pl.ds(pos_ref[0, 0, r], 1), :], dst_ref.at[pl.ds(r, 1), :], sem)


def _combine_body(pos_ref, pos_next_ref, ys_ref, x_ref, gt_ref, fg_ref, o_ref, buf, sems, *, bs, tt, final_norm):
    tm = bs * tt
    nt = pl.num_programs(1)
    step = pl.program_id(0) * nt + pl.program_id(1)
    nsteps = pl.num_programs(0) * nt
    slot = step % 2

    def issue(p_ref, s):
        def start(r, carry):
            _row_copy_gather(ys_ref, buf.at[s], p_ref, sems.at[s], r).start()
            return carry
        lax.fori_loop(0, tm, start, 0, unroll=DMA_UNROLL)

    @pl.when(step == 0)
    def _():
        issue(pos_ref, 0)

    @pl.when(step + 1 < nsteps)
    def _():
        issue(pos_next_ref, 1 - slot)

    def wait(r, carry):
        _row_copy_gather(ys_ref, buf.at[slot], pos_ref, sems.at[slot], r).wait()
        return carry

    lax.fori_loop(0, tm, wait, 0, unroll=DMA_UNROLL)

    left, right = _unpack_rows(buf[slot])
    y = jnp.concatenate([left, right], axis=1).reshape(bs, tt, D_MODEL)
    xn = _gated_add(x_ref[...], gt_ref[...], y)
    if final_norm:
        x2 = xn.reshape(tm, D_MODEL)
        xn = (x2 * _inv_rms(x2) * fg_ref[...]).reshape(bs, tt, D_MODEL)
    o_ref[...] = xn


def _combine(ys, pos3, x, gt, final_g, *, bs, tt, final_norm):
    bsz, seq, d = x.shape
    tm = bs * tt
    nt = seq // tt
    nsteps = (bsz // bs) * nt
    body = functools.partial(_combine_body, bs=bs, tt=tt, final_norm=final_norm)
    return pl.pallas_call(
        body,
        grid=(bsz // bs, nt),
        in_specs=[
            pl.BlockSpec((1, 1, tm), lambda b, t: (b * nt + t, 0, 0), memory_space=pltpu.SMEM),
            pl.BlockSpec((1, 1, tm), lambda b, t: (jnp.minimum(b * nt + t + 1, nsteps - 1), 0, 0),
                         memory_space=pltpu.SMEM),
            pl.BlockSpec(memory_space=pl.ANY),
            pl.BlockSpec((bs, tt, d), lambda b, t: (b, t, 0)),
            pl.BlockSpec((bs, SUBLANES, d), lambda b, t: (b, 0, 0)),
            pl.BlockSpec((1, d), lambda b, t: (0, 0)),
        ],
        out_specs=pl.BlockSpec((bs, tt, d), lambda b, t: (b, t, 0)),
        out_shape=jax.ShapeDtypeStruct((bsz, seq, d), f32),
        scratch_shapes=[pltpu.VMEM((2, tm, HALF), u32), pltpu.SemaphoreType.DMA((2,))],
        compiler_params=_cparams(("arbitrary", "arbitrary")),
        name="moe_combine",
    )(pos3, pos3, ys, x, gt, final_g)


def _seg_sum(x, ones_bd):
    hi, lo = _split2(x)
    return _dot(hi, ones_bd) + _dot(lo, ones_bd)


def _rwkv_proj_body(x_ref, sh_ref, sc_ref, g_ref, mu_ref, shift_ref,
                    wr_ref, wk_ref, wv_ref, w1_ref, a1_ref, g1_ref, w2_ref, a2_ref, g2_ref,
                    vec_ref, ones_ref,
                    r_o, k_o, v_o, kk_o, b_o, g_o, lw_o, sh_o,
                    hs_scr, mix_scr, l1w_scr, l1a_scr, l1g_scr, *, bs, tt, tn):
    tm = bs * tt
    t = pl.program_id(1)
    j = pl.program_id(2)
    prev_row = CARRY_ROWS - 1

    @pl.when(j == 0)
    def _():
        @pl.when(t == 0)
        def _():
            hs_scr[:, prev_row:CARRY_ROWS, :] = shift_ref[...]

        h = _norm_mod(x_ref[...], g_ref[...], sh_ref[...], sc_ref[...])
        hs_scr[:, CARRY_ROWS:, :] = h
        h_prev = hs_scr[:, prev_row:prev_row + tt, :]
        h4 = h.reshape(bs, tt // SUBLANES, SUBLANES, D_MODEL)
        xx4 = h_prev.reshape(h4.shape) - h4
        for m in range(N_SHIFT_MIX):
            mix_scr[m] = (h4 + xx4 * mu_ref[m]).reshape(tm, D_MODEL).astype(bf16)
        last = hs_scr[:, prev_row + tt:CARRY_ROWS + tt, :]
        hs_scr[:, prev_row:CARRY_ROWS, :] = last
        sh_o[...] = last
        l1w_scr[...] = jnp.tanh(_dot(mix_scr[1], w1_ref[...])).astype(bf16)
        l1a_scr[...] = _dot(mix_scr[4], a1_ref[...]).astype(bf16)
        l1g_scr[...] = _sigmoid(_dot(mix_scr[5], g1_ref[...])).astype(bf16)

    vec = vec_ref[...]
    w0, a0, k_k, k_a = (vec[i * SUBLANES:(i + 1) * SUBLANES] for i in range(4))
    tiles = lambda x: x.reshape(tm // SUBLANES, SUBLANES, tn)
    r = _dot(mix_scr[0], wr_ref[...])
    k = tiles(_dot(mix_scr[2], wk_ref[...]))
    v = _dot(mix_scr[3], wv_ref[...])
    wl = w0 + tiles(_dot(l1w_scr[...], w2_ref[...]))
    a = _sigmoid(a0 + tiles(_dot(l1a_scr[...], a2_ref[...])))
    g = _dot(l1g_scr[...], g2_ref[...])
    neg = -wl
    softplus = jnp.maximum(neg, 0.0) + jnp.log(1.0 + jnp.exp(-jnp.abs(neg)))
    w_log = -softplus - 0.5
    lw = -jnp.exp(w_log)
    kkr = k * k_k
    ss = tiles(_dot((kkr * kkr).reshape(tm, tn).astype(bf16), ones_ref[...]))
    kk = kkr * lax.rsqrt(jnp.maximum(ss, 1e-24))
    k2 = k * (1.0 + (a - 1.0) * k_a)
    shp = (bs, tt, tn)
    r_o[...] = r.reshape(shp).astype(bf16)
    k_o[...] = k2.reshape(shp).astype(bf16)
    v_o[...] = v.reshape(shp).astype(bf16)
    kk_o[...] = kk.reshape(shp).astype(bf16)
    b_o[...] = (kk * a).reshape(shp).astype(bf16)
    g_o[...] = g.reshape(shp).astype(bf16)
    lw_o[...] = lw.reshape(shp)


def _rwkv_proj(x, sh, sc, gain, mu, shift, wr, wk, wv, w1p, a1p, g1, w2p, a2p, g2, vec, ones_bd,
               *, bs, tt, tn=256):
    bsz, seq, d = x.shape
    tm = bs * tt
    dg = g1.shape[1]
    body = functools.partial(_rwkv_proj_body, bs=bs, tt=tt, tn=tn)
    const2 = lambda b, t, j: (0, 0)
    colblk = lambda b, t, j: (0, j)
    tok = lambda b, t, j: (b, t, j)
    act = lambda dt: jax.ShapeDtypeStruct((bsz, seq, d), dt)
    return pl.pallas_call(
        body,
        grid=(bsz // bs, seq // tt, d // tn),
        in_specs=[
            pl.BlockSpec((bs, tt, d), lambda b, t, j: (b, t, 0)),
            pl.BlockSpec((bs, SUBLANES, d), lambda b, t, j: (b, 0, 0)),
            pl.BlockSpec((bs, SUBLANES, d), lambda b, t, j: (b, 0, 0)),
            pl.BlockSpec((1, d), const2),
            pl.BlockSpec((N_SHIFT_MIX, SUBLANES, d), lambda b, t, j: (0, 0, 0)),
            pl.BlockSpec((bs, 1, d), lambda b, t, j: (b, 0, 0)),
            pl.BlockSpec((d, tn), colblk), pl.BlockSpec((d, tn), colblk), pl.BlockSpec((d, tn), colblk),
            pl.BlockSpec((d, LORA_PAD), const2), pl.BlockSpec((d, LORA_PAD), const2),
            pl.BlockSpec((d, dg), const2),
            pl.BlockSpec((LORA_PAD, tn), colblk), pl.BlockSpec((LORA_PAD, tn), colblk),
            pl.BlockSpec((dg, tn), colblk),
            pl.BlockSpec((4 * SUBLANES, tn), colblk),
            pl.BlockSpec((tn, tn), const2),
        ],
        out_specs=[pl.BlockSpec((bs, tt, tn), tok)] * 7 + [pl.BlockSpec((bs, 1, d), lambda b, t, j: (b, 0, 0))],
        out_shape=[act(bf16)] * 6 + [act(f32), jax.ShapeDtypeStruct((bsz, 1, d), f32)],
        scratch_shapes=[
            pltpu.VMEM((bs, tt + CARRY_ROWS, d), f32),
            pltpu.VMEM((N_SHIFT_MIX, tm, d), bf16),
            pltpu.VMEM((tm, LORA_PAD), bf16),
            pltpu.VMEM((tm, LORA_PAD), bf16),
            pltpu.VMEM((tm, dg), bf16),
        ],
        compiler_params=_cparams(("arbitrary", "arbitrary", "arbitrary")),
        name="rwkv_proj",
    )(x, sh, sc, gain, mu, shift, wr, wk, wv, w1p, a1p, g1, w2p, a2p, g2, vec, ones_bd)


def _block_diag_rows(x, head0):
    return jnp.concatenate([jnp.where(head0, x, 0.0), jnp.where(head0, 0.0, x)], axis=1).astype(bf16)


def _bmm(a, b):
    return lax.dot_general(a, b, (((2,), (1,)), ((0,), (0,))), preferred_element_type=f32)


def _bmm_nt(a, b):
    return lax.dot_general(a, b, (((2,), (2,)), ((0,), (0,))), preferred_element_type=f32)


def _wkv_chunk(r, k, v, kk, beta, lw, c, c_end, state, masks):
    head0, strict, incl, eye, bd_mask = masks
    n = WKV_CHUNK
    bd = lambda x: _block_diag_rows(x, head0)
    g_inv = jnp.exp(-c)
    g_end = jnp.exp(c_end - c)
    a_t = -(kk * jnp.exp(c - lw))
    r_t = r * jnp.exp(c)
    ar = jnp.concatenate([a_t, r_t], axis=1).astype(bf16)
    a1 = _bmm_nt(ar, bd(beta * g_inv))
    a2 = _bmm_nt(ar, bd(k * g_inv))
    a_ab = jnp.where(strict, a1[:, :n], 0.0)
    a_rb = jnp.where(incl, a1[:, n:], 0.0)
    a_ak = jnp.where(strict, a2[:, :n], 0.0)
    a_rk = jnp.where(incl, a2[:, n:], 0.0)
    v_bd = bd(v)
    av = _bmm(a_ak.astype(bf16), v_bd)
    ov = _bmm(a_rk.astype(bf16), v_bd)
    q = a_ab
    p = eye + q
    q = _bmm(q.astype(bf16), bd(q))
    for level in range(5):
        if level < 4:
            res = _bmm(jnp.concatenate([p, q], axis=1).astype(bf16), bd(q))
            p = p + res[:, :n]
            q = res[:, n:]
        else:
            p = p + _bmm(p.astype(bf16), bd(q))
    p_bf = p.astype(bf16)
    a_bar = _bmm(p_bf, bd(a_t))
    u0 = _bmm(p_bf, bd(av))
    s_bf = state.astype(bf16)
    from_state = _bmm_nt(jnp.concatenate([a_bar, r_t], axis=1).astype(bf16), s_bf)
    u = from_state[:, :n] + u0
    o = from_state[:, n:] + ov + _bmm(a_rb.astype(bf16), bd(u))
    uv_t = jnp.swapaxes(jnp.concatenate([u, v], axis=1), 1, 2).astype(bf16)
    bk = jnp.concatenate([beta * g_end, k * g_end], axis=1).astype(bf16)
    decay = jnp.exp(c_end)
    new_state = state * jnp.concatenate([decay, decay], axis=1) + jnp.where(bd_mask, _bmm(uv_t, bk), 0.0)
    return o, new_state


def _wkv_body(r_ref, k_ref, v_ref, kk_ref, b_ref, g_ref, lw_ref, s0_ref, vec_ref, tril_ref, ones_ref,
              z_ref, so_ref, s_scr, *, tc):
    t = pl.program_id(1)
    n = WKV_CHUNK

    @pl.when(t == 0)
    def _():
        s_scr[...] = s0_ref[0]

    lane = lax.broadcasted_iota(i32, (n, PAIR), 1)
    row = lax.broadcasted_iota(i32, (n, PAIR), 0)
    head0 = lane < HEAD_SIZE
    within = lane & (HEAD_SIZE - 1)
    strict = within < row
    incl = within <= row
    eye = (within == row).astype(f32)
    bd_mask = ((lax.broadcasted_iota(i32, (PAIR, PAIR), 0) < HEAD_SIZE)
               == (lax.broadcasted_iota(i32, (PAIR, PAIR), 1) < HEAD_SIZE))
    masks = (head0, strict, incl, eye, bd_mask)
    tril = tril_ref[...]
    ones_bd = ones_ref[...]
    rows = min(tc, n)
    nchunks = max(tc // n, 1)

    def units(x):
        return jnp.stack([x[:, p * PAIR:(p + 1) * PAIR] for p in range(N_PAIRS)], axis=0)

    def seg_sum(x):
        flat = x.reshape(N_PAIRS * n, PAIR).astype(bf16)
        return _dot(flat, ones_bd).reshape(N_PAIRS, n, PAIR)

    vec = units(vec_ref[...])
    ln_w, ln_b, r_k = (vec[:, i * SUBLANES:(i + 1) * SUBLANES] for i in range(3))

    def load(ref, ci):
        x = ref[0, ci * n:ci * n + rows, :].astype(f32)
        if rows < n:
            x = jnp.concatenate([x, jnp.zeros((n - rows, x.shape[1]), f32)], axis=0)
        return x

    for ci in range(nchunks):
        lw = load(lw_ref, ci)
        hi = lw.astype(bf16)
        rem = lw - hi.astype(f32)
        mid = rem.astype(bf16)
        lo = (rem - mid.astype(f32)).astype(bf16)
        sums = _dot(tril, hi) + _dot(tril, mid) + _dot(tril, lo)
        r, k, v = (units(load(ref, ci)) for ref in (r_ref, k_ref, v_ref))
        o, s_new = _wkv_chunk(r, k, v, units(load(kk_ref, ci)), units(load(b_ref, ci)), units(lw),
                              units(sums[:n]), units(sums[n:]), s_scr[...], masks)
        s_scr[...] = s_new
        mean = seg_sum(o) * (1.0 / HEAD_SIZE)
        dev = o - mean
        var = seg_sum(dev * dev) * (1.0 / HEAD_SIZE)
        y = _per_seq(dev * lax.rsqrt(var + GN_EPS), lambda y4, w4, b4: y4 * w4 + b4, ln_w, ln_b)
        y = y + seg_sum(_per_seq(r * k, lambda x4, rk4: x4 * rk4, r_k)) * v
        z = (y * units(load(g_ref, ci))).astype(bf16)
        z_ref[0, ci * n:ci * n + rows, :] = jnp.concatenate([z[p, :rows] for p in range(N_PAIRS)], axis=1)

    @pl.when(t == pl.num_programs(1) - 1)
    def _():
        so_ref[0] = s_scr[...]


def _wkv(r, k, v, kk, beta, g, lw, state_bd, vec, tril, ones_bd, *, tc):
    bsz, seq, d = r.shape
    body = functools.partial(_wkv_body, tc=tc)
    tok = pl.BlockSpec((1, tc, d), lambda b, t: (b, t, 0))
    st = pl.BlockSpec((1, N_PAIRS, PAIR, PAIR), lambda b, t: (b, 0, 0, 0))
    return pl.pallas_call(
        body,
        grid=(bsz, seq // tc),
        in_specs=[tok] * 7 + [
            st,
            pl.BlockSpec((3 * SUBLANES, d), lambda b, t: (0, 0)),
            pl.BlockSpec((2 * WKV_CHUNK, WKV_CHUNK), lambda b, t: (0, 0)),
            pl.BlockSpec((PAIR, PAIR), lambda b, t: (0, 0)),
        ],
        out_specs=[tok, st],
        out_shape=[jax.ShapeDtypeStruct((bsz, seq, d), bf16),
                   jax.ShapeDtypeStruct(state_bd.shape, f32)],
        scratch_shapes=[pltpu.VMEM((N_PAIRS, PAIR, PAIR), f32)],
        compiler_params=_cparams(("arbitrary", "arbitrary")),
        name="wkv",
    )(r, k, v, kk, beta, g, lw, state_bd, vec, tril, ones_bd)


def _block_ones(n):
    idx = jnp.arange(n) // HEAD_SIZE
    return (idx[:, None] == idx[None, :]).astype(bf16)


def _state_to_pairs(s):
    b = s.shape[0]
    s5 = s.reshape(b, N_PAIRS, 2, HEAD_SIZE, HEAD_SIZE)
    eye = jnp.eye(2, dtype=s.dtype)
    bd = s5[:, :, :, :, None, :] * eye[None, None, :, None, :, None]
    return bd.reshape(b, N_PAIRS, PAIR, PAIR)


def _pairs_to_state(bd):
    b = bd.shape[0]
    s6 = bd.reshape(b, N_PAIRS, 2, HEAD_SIZE, 2, HEAD_SIZE)
    s = jnp.stack([s6[:, :, 0, :, 0, :], s6[:, :, 1, :, 1, :]], axis=2)
    return s.reshape(b, N_HEADS, HEAD_SIZE, HEAD_SIZE)


_PAIR_TABLE = ((0, 1), (0, 2), (0, 3), (1, 2), (1, 3), (2, 3))


def _moe_plan(counts, n_tiles, tm):
    cnt = counts[:N_CLASSES].astype(i32)
    tiles_per = (cnt + tm - 1) // tm
    tile_end = jnp.cumsum(tiles_per)
    tile_start = tile_end - tiles_per
    offsets = tile_start * tm
    tidx = jnp.arange(n_tiles, dtype=i32)
    cls_of_tile = jnp.sum((tidx[:, None] >= tile_end[None, :]).astype(i32), axis=1)
    valid = (cls_of_tile < N_CLASSES).astype(i32)
    last_cls = jnp.max(jnp.where(tiles_per > 0, jnp.arange(N_CLASSES, dtype=i32), 0))
    cls_c = jnp.where(valid > 0, cls_of_tile, last_cls)
    grp = cls_c // 6
    pr = cls_c % 6
    lo_tab = jnp.array([p[0] for p in _PAIR_TABLE], i32)
    hi_tab = jnp.array([p[1] for p in _PAIR_TABLE], i32)
    ea = grp * EXPERTS_PER_GROUP + lo_tab[pr]
    eb = grp * EXPERTS_PER_GROUP + hi_tab[pr]
    return offsets, ea, eb, valid


def _positions(info, offsets):
    cls = info[:, 0, :].reshape(-1)
    rank = info[:, 1, :].reshape(-1)
    return offsets[cls] + rank


def _router_weights(wg, bg, we, be):
    w = jnp.concatenate([wg, we], axis=1).T
    w = jnp.pad(w, ((0, ROUTER_ROWS - w.shape[0]), (0, 0)))
    hi, lo = _split2_outside(w)
    bias = jnp.pad(jnp.concatenate([bg, be]), (0, ROUTER_ROWS - N_GROUPS - N_EXPERTS))
    return jnp.concatenate([hi, lo], axis=0), bias.reshape(ROUTER_ROWS, 1)


def _tiles(bsz, seq, rows):
    tt = min(seq, rows)
    bs = max(rows // tt, 1)
    assert bsz % bs == 0 and seq % tt == 0
    return bs, tt


def kernel(x_prompt, x_sample, c_prompt, c_sample, state_conv, state_shift, state_wkv, ada_w, ada_b, norm_g, final_g, cv_in, cv_w, cv_out, rw_mu, rw_r, rw_k, rw_v, rw_o, rw_w0, rw_w1, rw_w2, rw_a0, rw_a1, rw_a2, rw_g1, rw_g2, rw_kk, rw_ka, rw_rk, rw_lnw, rw_lnb, moe_wg, moe_bg, moe_we, moe_be, moe_w1, moe_w3, moe_w2):
    d = D_MODEL
    xs = [x_prompt, x_sample]
    bszs = [x.shape[0] for x in xs]
    seqs = [x.shape[1] for x in xs]
    ntok = [b * s for b, s in zip(bszs, seqs)]
    n_total = sum(ntok)

    c_all = jnp.concatenate([c_prompt, c_sample], axis=0)
    rows_c = c_all.shape[0]
    rows_pad = -(-rows_c // 8) * 8
    mod = _ada(jnp.pad(c_all, ((0, rows_pad - rows_c), (0, 0))), ada_w, ada_b)

    def mods(layer, trunk):
        lo = 0 if trunk == 0 else bszs[0]
        m = mod[layer, lo:lo + bszs[trunk]]
        return [jnp.broadcast_to(m[:, None, i * d:(i + 1) * d], (bszs[trunk], SUBLANES, d))
                for i in range(N_ADA)]

    conv_states = [jnp.zeros((1, bszs[0], CONV_WIDTH - 1, d), f32), state_conv]
    shift_states = [jnp.zeros((1, bszs[0], d), f32), state_shift]
    wkv_states = [jnp.zeros((1, bszs[0], N_HEADS, HEAD_SIZE, HEAD_SIZE), f32), state_wkv]

    tri_cache = {}

    def tri(n):
        if n not in tri_cache:
            tri_cache[n] = (jnp.arange(n)[:, None] < jnp.arange(n)[None, :]).astype(bf16)
        return tri_cache[n]

    n_tiles = n_total // MOE_TILE + N_CLASSES
    p_rows = n_tiles * MOE_TILE

    def moe_layer(layer, rows_list, info_list, counts):
        offsets, ea, eb, valid = _moe_plan(counts[:, 0], n_tiles, MOE_TILE)
        pos_list = [_positions(info, offsets) for info in info_list]
        sorted_rows = jnp.zeros((p_rows, ROW_WORDS), u32)
        for rows, pos in zip(rows_list, pos_list):
            sorted_rows = _dispatch(rows, pos.reshape(-1, 1, MOE_TILE), sorted_rows, tm=MOE_TILE)
        ys = _moe(sorted_rows, ea, eb, valid, moe_w1[layer].astype(bf16), moe_w3[layer].astype(bf16),
                  moe_w2[layer].astype(bf16), tm=MOE_TILE)
        return ys, pos_list

    def out_router_pair(layer, acts, w_bf, xcur, tails=(None, None)):
        wrt, rbias = _router_weights(moe_wg[layer], moe_bg[layer], moe_we[layer], moe_be[layer])
        counts = jnp.zeros((CLASS_ROWS, 128), f32)
        x_new, rows_list, info_list = [], [], []
        for trunk in range(2):
            _, _, gt_m, sh_f, sc_f, _ = mods(layer, trunk)
            bs, tt = _tiles(bszs[trunk], seqs[trunk], ROUTER_TILE)
            xo, rows, info, counts = _out_router(
                acts[trunk], w_bf, xcur[trunk], gt_m, norm_g[layer, 1].reshape(1, d), sh_f, sc_f,
                wrt, rbias, counts, tri(bs * tt), bs=bs, tt=tt, tail=tails[trunk])
            x_new.append(xo)
            rows_list.append(rows)
            info_list.append(info)
        return x_new, rows_list, info_list, counts

    def combine_pair(layer, ys, pos_list, xcur, final):
        out = []
        for trunk in range(2):
            gt_f = mods(layer, trunk)[5]
            bs, tt = _tiles(bszs[trunk], seqs[trunk], COMBINE_TILE)
            out.append(_combine(ys, pos_list[trunk].reshape(-1, 1, bs * tt), xcur[trunk], gt_f,
                                final_g.reshape(1, d), bs=bs, tt=tt, final_norm=final))
        return out

    w_in_hi, w_in_lo = _split2_outside(cv_in[0])
    w_out_hi, w_out_lo = _split2_outside(cv_out[0])
    gain0 = norm_g[0, 0].reshape(1, d)
    conv_taps = jnp.repeat(cv_w[0], SUBLANES, axis=0)
    z0, conv_out = [], []
    for trunk in range(2):
        sh_m, sc_m = mods(0, trunk)[:2]
        bs, tt = _tiles(bszs[trunk], seqs[trunk], PROJ_TILE)
        st8 = jnp.pad(conv_states[trunk][0], ((0, 0), (CARRY_ROWS - (CONV_WIDTH - 1), 0), (0, 0)))
        z, so = _conv_in(xs[trunk], sh_m, sc_m, gain0, cv_in[0].astype(bf16), conv_taps, st8, bs=bs, tt=tt)
        z0.append(z)
        conv_out.append(so[:, -1, CARRY_ROWS - (CONV_WIDTH - 1):, :][None])
    assert min(seqs) >= TAIL_WINDOW
    x_tail = jnp.concatenate([x[:, -TAIL_WINDOW:, :] for x in xs], axis=0)
    n_seq = x_tail.shape[0]
    mod_tail = [jnp.concatenate([mods(0, trunk)[i] for trunk in range(2)], axis=0) for i in range(3)]
    z_tail, _ = _conv_in(x_tail, mod_tail[0], mod_tail[1], gain0, w_in_hi, conv_taps,
                         jnp.zeros((n_seq, CARRY_ROWS, d), f32), bs=n_seq, tt=TAIL_WINDOW, w_in_lo=w_in_lo)
    x1_tail = _tail_out(z_tail, x_tail, mod_tail[2], w_out_hi, w_out_lo)[:, TAIL_WINDOW - TAIL_ROWS:, :]
    tails = (x1_tail[:bszs[0]], x1_tail[bszs[0]:])
    x1, rows0, info0, counts0 = out_router_pair(0, z0, cv_out[0].astype(bf16), xs, tails)
    ys0, pos0 = moe_layer(0, rows0, info0, counts0)
    x2 = combine_pair(0, ys0, pos0, x1, False)

    pad_l = ((0, 0), (0, LORA_PAD - rw_w1.shape[2]))
    pad_r = ((0, LORA_PAD - rw_w2.shape[1]), (0, 0))
    w1p = jnp.pad(rw_w1[0], pad_l).astype(bf16)
    a1p = jnp.pad(rw_a1[0], pad_l).astype(bf16)
    w2p = jnp.pad(rw_w2[0], pad_r).astype(bf16)
    a2p = jnp.pad(rw_a2[0], pad_r).astype(bf16)
    vec_p = jnp.repeat(jnp.stack([rw_w0[0], rw_a0[0], rw_kk[0], rw_ka[0]]), SUBLANES, axis=0)
    vec_w = jnp.repeat(jnp.stack([rw_lnw[0], rw_lnb[0], rw_rk[0].reshape(d)]), SUBLANES, axis=0)
    wr_bf, wk_bf, wv_bf = rw_r[0].astype(bf16), rw_k[0].astype(bf16), rw_v[0].astype(bf16)
    g1_bf, g2_bf = rw_g1[0].astype(bf16), rw_g2[0].astype(bf16)
    proj_tn = 256
    tril = jnp.concatenate([jnp.arange(WKV_CHUNK)[:, None] >= jnp.arange(WKV_CHUNK)[None, :],
                            jnp.ones((WKV_CHUNK, WKV_CHUNK), bool)], axis=0).astype(bf16)
    z1, shift_out, wkv_out = [], [], []
    for trunk in range(2):
        sh_m, sc_m = mods(1, trunk)[:2]
        bs, tt = _tiles(bszs[trunk], seqs[trunk], PROJ_TILE)
        r, k, v, kk, beta, g, lw, sho = _rwkv_proj(
            x2[trunk], sh_m, sc_m, norm_g[1, 0].reshape(1, d),
            jnp.broadcast_to(rw_mu[0][:, None, :], (N_SHIFT_MIX, SUBLANES, d)),
            shift_states[trunk][0].reshape(bszs[trunk], 1, d), wr_bf, wk_bf, wv_bf, w1p, a1p, g1_bf,
            w2p, a2p, g2_bf, vec_p, _block_ones(proj_tn), bs=bs, tt=tt, tn=proj_tn)
        tc = min(seqs[trunk], 2 * WKV_CHUNK)
        z, s_bd = _wkv(r, k, v, kk, beta, g, lw, _state_to_pairs(wkv_states[trunk][0]), vec_w, tril,
                       _block_ones(PAIR), tc=tc)
        z1.append(z)
        shift_out.append(sho.reshape(1, bszs[trunk], d))
        wkv_out.append(_pairs_to_state(s_bd)[None])
    x3, rows1, info1, counts1 = out_router_pair(1, z1, rw_o[0].astype(bf16), x2)
    ys1, pos1 = moe_layer(1, rows1, info1, counts1)
    y = combine_pair(1, ys1, pos1, x3, True)

    return (y[0], y[1], conv_out[0], shift_out[0], wkv_out[0], conv_out[1], shift_out[1], wkv_out[1])
```

```python
import functools

import jax
import jax.numpy as jnp
from jax import lax
from jax.experimental import pallas as pl
from jax.experimental.pallas import tpu as pltpu

f32 = jnp.float32
bf16 = jnp.bfloat16
i32 = jnp.int32
u32 = jnp.uint32

D_MODEL = 2048
LANES = 128
SUBLANES = 8
HEAD_SIZE = 64
N_HEADS = D_MODEL // HEAD_SIZE
WKV_HEADS = 2
PAIR = WKV_HEADS * HEAD_SIZE
N_PAIRS = D_MODEL // PAIR
CONV_WIDTH = 3
N_GROUPS = 4
EXPERTS_PER_GROUP = 4
N_EXPERTS = N_GROUPS * EXPERTS_PER_GROUP
D_EXPERT = D_MODEL // 4
N_ADA = 6
N_SHIFT_MIX = 6
RMS_EPS = 1e-6
GN_EPS = 64e-5
LORA_PAD = 128
N_CLASSES = N_GROUPS * 6
CLASS_ROWS = 32
ROUTER_ROWS = 128
HALF = D_MODEL // 2
ROW_WORDS = HALF + 128
WKV_CHUNK = 64
CARRY_ROWS = 8
TAIL_ROWS = 8
TAIL_WINDOW = 2 * TAIL_ROWS

V7X_VMEM_LIMIT = 56 * 1024 * 1024
MOE_TILE = 512
DMA_UNROLL = 8
ROUTER_TILE = 512
COMBINE_TILE = 512
PROJ_TILE = 512


def _cparams(sem, vmem=V7X_VMEM_LIMIT):
    return pltpu.CompilerParams(dimension_semantics=sem, vmem_limit_bytes=vmem)


def _dot(a, b):
    return jnp.dot(a, b, preferred_element_type=f32)


def _dot_nt(a, b):
    return lax.dot_general(a, b, (((1,), (1,)), ((), ())), preferred_element_type=f32)


def _dot_tn(a, b):
    return lax.dot_general(a, b, (((0,), (0,)), ((), ())), preferred_element_type=f32)


def _split2(x):
    hi = pltpu.bitcast(pltpu.bitcast(x, u32) & jnp.uint32(0xFFFF0000), f32)
    return hi.astype(bf16), (x - hi).astype(bf16)


def _split2_outside(x):
    hi = lax.bitcast_convert_type(lax.bitcast_convert_type(x, u32) & jnp.uint32(0xFFFF0000), f32)
    return hi.astype(bf16), (x - hi).astype(bf16)


def _sigmoid(x):
    return 1.0 / (1.0 + jnp.exp(-x))


def _pack_rows(x):
    hi = pltpu.bitcast(x[:, :HALF].astype(bf16).astype(f32), u32)
    lo = pltpu.bitcast(x[:, HALF:].astype(bf16).astype(f32), u32)
    return (hi & jnp.uint32(0xFFFF0000)) | (lo >> 16)


def _unpack_rows(p):
    left = pltpu.bitcast(p & jnp.uint32(0xFFFF0000), f32)
    right = pltpu.bitcast(p << 16, f32)
    return left, right


def _dot3(a_hi, a_lo, w_hi, w_lo):
    return _dot(a_hi, w_hi) + _dot(a_lo, w_hi) + _dot(a_hi, w_lo)


def _ada_body(c_ref, w_ref, b_ref, o_ref):
    c = c_ref[...]
    s_hi, s_lo = _split2(c * _sigmoid(c))
    w_hi, w_lo = _split2(w_ref[0])
    o_ref[0] = _dot3(s_hi, s_lo, w_hi, w_lo) + b_ref[0]


def _ada(c_all, ada_w, ada_b, tn=1024):
    depth, d, n = ada_w.shape
    rows = c_all.shape[0]
    return pl.pallas_call(
        _ada_body,
        grid=(depth, n // tn),
        in_specs=[
            pl.BlockSpec((rows, d), lambda l, j: (0, 0)),
            pl.BlockSpec((1, d, tn), lambda l, j: (l, 0, j)),
            pl.BlockSpec((1, 1, tn), lambda l, j: (l, 0, j)),
        ],
        out_specs=pl.BlockSpec((1, rows, tn), lambda l, j: (l, 0, j)),
        out_shape=jax.ShapeDtypeStruct((depth, rows, n), f32),
        compiler_params=_cparams(("arbitrary", "arbitrary")),
        name="ada",
    )(c_all, ada_w, ada_b.reshape(depth, 1, n))


def _inv_rms(x2):
    sq = x2 * x2
    part = sq[:, 0:LANES]
    for i in range(1, D_MODEL // LANES):
        part = part + sq[:, i * LANES:(i + 1) * LANES]
    hi, lo = _split2(part)
    ones = jnp.ones((LANES, LANES), bf16)
    ms = (_dot(hi, ones) + _dot(lo, ones)) * (1.0 / D_MODEL)
    rs = lax.rsqrt(ms + RMS_EPS)
    return jnp.concatenate([rs] * (D_MODEL // LANES), axis=1)


def _per_seq(x, fn, *vecs):
    bs, tt, n = x.shape
    x4 = x.reshape(bs, tt // SUBLANES, SUBLANES, n)
    return fn(x4, *(v[:, None] for v in vecs)).reshape(bs, tt, n)


def _gated_add(x, gate, y):
    return _per_seq(y, lambda y4, g4: g4 * y4, gate) + x


def _norm_mod(x, gain, sh, sc):
    bs, tt, d = x.shape
    x2 = x.reshape(bs * tt, d)
    y = (x2 * _inv_rms(x2) * gain).reshape(bs, tt, d)
    return _per_seq(y, lambda y4, sh4, sc4: y4 * (1.0 + sc4) + sh4, sh, sc)


def _conv_in_body(x_ref, sh_ref, sc_ref, g_ref, cw_ref, st_ref, *rest, bs, tt, tn, precise):
    nw = 6 if precise else 3
    w_refs = rest[:nw]
    z_ref, so_ref, h_scr, carry_scr, uext_scr = rest[nw:]
    t = pl.program_id(1)
    j = pl.program_id(2)

    @pl.when(j == 0)
    def _():
        h = _norm_mod(x_ref[...], g_ref[...], sh_ref[...], sc_ref[...]).reshape(bs * tt, D_MODEL)
        if precise:
            h_scr[0], h_scr[1] = _split2(h)
        else:
            h_scr[0] = h.astype(bf16)

    @pl.when(t == 0)
    def _():
        carry_scr[j] = st_ref[...]

    if precise:
        b_gate, c_gate, xv = (_dot3(h_scr[0], h_scr[1], w_refs[i][...], w_refs[i + 3][...]) for i in range(3))
    else:
        b_gate, c_gate, xv = (_dot(h_scr[0], w_refs[i][...]) for i in range(3))
    u = (c_gate * xv).reshape(bs, tt, tn)
    uext_scr[:, 0:CARRY_ROWS, :] = carry_scr[j]
    uext_scr[:, CARRY_ROWS:, :] = u
    cw = cw_ref[...]
    tap = lambda i: cw[i * SUBLANES:(i + 1) * SUBLANES]
    tiles = lambda x: x.reshape(bs, tt // SUBLANES, SUBLANES, tn)
    conv = (tiles(uext_scr[:, CARRY_ROWS - 2:CARRY_ROWS - 2 + tt, :]) * tap(0)
            + tiles(uext_scr[:, CARRY_ROWS - 1:CARRY_ROWS - 1 + tt, :]) * tap(1)
            + tiles(u) * tap(2))
    z_ref[...] = (b_gate.reshape(bs, tt, tn) * conv.reshape(bs, tt, tn)).astype(z_ref.dtype)
    last = uext_scr[:, tt:tt + CARRY_ROWS, :]
    carry_scr[j] = last
    so_ref[:, 0] = last


def _conv_in(x, sh, sc, gain, w_in_hi, conv_w, state8, *, bs, tt, tn=512, w_in_lo=None):
    bsz, seq, d = x.shape
    nj = d // tn
    precise = w_in_lo is not None
    body = functools.partial(_conv_in_body, bs=bs, tt=tt, tn=tn, precise=precise)
    w_specs = [pl.BlockSpec((d, tn), lambda b, t, j, k=k: (0, j + k * nj)) for k in range(3)]
    weights = [w_in_hi] * 3 + ([w_in_lo] * 3 if precise else [])
    return pl.pallas_call(
        body,
        grid=(bsz // bs, seq // tt, nj),
        in_specs=[
            pl.BlockSpec((bs, tt, d), lambda b, t, j: (b, t, 0)),
            pl.BlockSpec((bs, SUBLANES, d), lambda b, t, j: (b, 0, 0)),
            pl.BlockSpec((bs, SUBLANES, d), lambda b, t, j: (b, 0, 0)),
            pl.BlockSpec((1, d), lambda b, t, j: (0, 0)),
            pl.BlockSpec((CONV_WIDTH * SUBLANES, tn), lambda b, t, j: (0, j)),
            pl.BlockSpec((bs, CARRY_ROWS, tn), lambda b, t, j: (b, 0, j)),
        ] + w_specs * (2 if precise else 1),
        out_specs=[
            pl.BlockSpec((bs, tt, tn), lambda b, t, j: (b, t, j)),
            pl.BlockSpec((bs, 1, CARRY_ROWS, tn), lambda b, t, j: (b, t, 0, j)),
        ],
        out_shape=[
            jax.ShapeDtypeStruct((bsz, seq, d), f32 if precise else bf16),
            jax.ShapeDtypeStruct((bsz, seq // tt, CARRY_ROWS, d), f32),
        ],
        scratch_shapes=[
            pltpu.VMEM((2 if precise else 1, bs * tt, d), bf16),
            pltpu.VMEM((nj, bs, CARRY_ROWS, tn), f32),
            pltpu.VMEM((bs, tt + CARRY_ROWS, tn), f32),
        ],
        compiler_params=_cparams(("arbitrary", "arbitrary", "arbitrary")),
        name="conv_in_precise" if precise else "conv_in",
    )(x, sh, sc, gain, conv_w, state8, *weights)


def _tail_out_body(z_ref, x_ref, gt_ref, wh_ref, wl_ref, o_ref, *, bs, tt):
    z_hi, z_lo = _split2(z_ref[...].reshape(bs * tt, D_MODEL))
    y = _dot3(z_hi, z_lo, wh_ref[...], wl_ref[...])
    tn = y.shape[1]
    o_ref[...] = _gated_add(x_ref[...], gt_ref[...], y.reshape(bs, tt, tn))


def _tail_out(z, x, gt, w_hi, w_lo, *, tn=512):
    bsz, tt, d = x.shape
    body = functools.partial(_tail_out_body, bs=bsz, tt=tt)
    return pl.pallas_call(
        body,
        grid=(d // tn,),
        in_specs=[
            pl.BlockSpec((bsz, tt, d), lambda j: (0, 0, 0)),
            pl.BlockSpec((bsz, tt, tn), lambda j: (0, 0, j)),
            pl.BlockSpec((bsz, SUBLANES, tn), lambda j: (0, 0, j)),
            pl.BlockSpec((d, tn), lambda j: (0, j)),
            pl.BlockSpec((d, tn), lambda j: (0, j)),
        ],
        out_specs=pl.BlockSpec((bsz, tt, tn), lambda j: (0, 0, j)),
        out_shape=jax.ShapeDtypeStruct((bsz, tt, d), f32),
        compiler_params=_cparams(("arbitrary",)),
        name="tail_out",
    )(z, x, gt, w_hi, w_lo)


def _route_rows(logit):
    lg = [logit[g:g + 1, :] for g in range(N_GROUPS)]
    le = [logit[N_GROUPS + e:N_GROUPS + e + 1, :] for e in range(N_EXPERTS)]
    gmax = jnp.maximum(jnp.maximum(lg[0], lg[1]), jnp.maximum(lg[2], lg[3]))
    gidx = jnp.where(lg[0] == gmax, 0, jnp.where(lg[1] == gmax, 1, jnp.where(lg[2] == gmax, 2, 3)))
    denom = (jnp.exp(lg[0] - gmax) + jnp.exp(lg[1] - gmax)
             + jnp.exp(lg[2] - gmax) + jnp.exp(lg[3] - gmax))
    p_grp = 1.0 / denom
    leg = [jnp.where(gidx == 0, le[i],
                     jnp.where(gidx == 1, le[EXPERTS_PER_GROUP + i],
                               jnp.where(gidx == 2, le[2 * EXPERTS_PER_GROUP + i],
                                         le[3 * EXPERTS_PER_GROUP + i])))
           for i in range(EXPERTS_PER_GROUP)]
    v1 = jnp.maximum(jnp.maximum(leg[0], leg[1]), jnp.maximum(leg[2], leg[3]))
    i1 = jnp.where(leg[0] == v1, 0, jnp.where(leg[1] == v1, 1, jnp.where(leg[2] == v1, 2, 3)))
    neg = jnp.float32(-jnp.inf)
    rest = [jnp.where(i1 == i, neg, leg[i]) for i in range(EXPERTS_PER_GROUP)]
    v2 = jnp.maximum(jnp.maximum(rest[0], rest[1]), jnp.maximum(rest[2], rest[3]))
    i2 = jnp.where((rest[0] == v2) & (i1 != 0), 0,
                   jnp.where((rest[1] == v2) & (i1 != 1), 1,
                             jnp.where((rest[2] == v2) & (i1 != 2), 2, 3)))
    s = jnp.exp(v2 - v1)
    w_first = p_grp / (1.0 + s)
    w_second = p_grp * s / (1.0 + s)
    i_lo = jnp.minimum(i1, i2)
    i_hi = jnp.maximum(i1, i2)
    pair_base = jnp.where(i_lo == 0, 0, jnp.where(i_lo == 1, 3, 5))
    cls = gidx * 6 + pair_base + (i_hi - i_lo - 1)
    g_lo = jnp.where(i1 < i2, w_first, w_second)
    g_hi = jnp.where(i1 < i2, w_second, w_first)
    return cls, g_lo, g_hi


def _out_router_body(a_ref, w_ref, x_ref, gt_ref, g_ref, sh_ref, sc_ref, wrt_ref, rb_ref,
                     cin_ref, tri_ref, *rest, bs, tt, with_tail):
    if with_tail:
        tail_ref, xo_ref, rows_ref, info_ref, cout_ref, cnt_scr = rest
    else:
        xo_ref, rows_ref, info_ref, cout_ref, cnt_scr = rest
    tm = bs * tt
    first = (pl.program_id(0) == 0) & (pl.program_id(1) == 0)

    @pl.when(first)
    def _():
        cnt_scr[...] = cin_ref[...]

    y = _dot(a_ref[...].reshape(tm, D_MODEL), w_ref[...])
    xo_ref[...] = _gated_add(x_ref[...], gt_ref[...], y.reshape(bs, tt, D_MODEL))
    if with_tail:
        @pl.when(pl.program_id(1) == pl.num_programs(1) - 1)
        def _():
            xo_ref[:, tt - TAIL_ROWS:, :] = tail_ref[...]
    xn = xo_ref[...]
    h = _norm_mod(xn, g_ref[...], sh_ref[...], sc_ref[...]).reshape(tm, D_MODEL)
    h_hi, h_lo = _split2(h)
    wrt = wrt_ref[...]
    p_hi = _dot_nt(wrt, h_hi)
    p_lo = _dot_nt(wrt[:ROUTER_ROWS], h_lo)
    logit = p_hi[:ROUTER_ROWS] + p_hi[ROUTER_ROWS:] + p_lo + rb_ref[...]
    cls, g_lo, g_hi = _route_rows(logit)

    crow = lax.broadcasted_iota(i32, (CLASS_ROWS, tm), 0)
    onehot = (crow == cls).astype(f32)
    before = _dot(onehot.astype(bf16), tri_ref[...])
    base = cnt_scr[:, 0:1]
    rank = jnp.sum(onehot * (before + base), axis=0, keepdims=True).astype(i32)
    cnt_new = cnt_scr[...] + jnp.sum(onehot, axis=1, keepdims=True)
    cnt_scr[...] = cnt_new
    cout_ref[...] = cnt_new

    irow = lax.broadcasted_iota(i32, (8, tm), 0)
    info_ref[0] = jnp.where(irow == 0, cls, jnp.where(irow == 1, rank, 0))

    grow = lax.broadcasted_iota(i32, (ROUTER_ROWS, tm), 0)
    gates_t = jnp.where(grow == 0, g_lo, jnp.where(grow == 1, g_hi, 0.0))
    gates = pltpu.bitcast(gates_t.T, u32)
    rows_ref[:, :HALF] = _pack_rows(h)
    rows_ref[:, HALF:] = gates


def _out_router(a, w_bf, x, gt, gain, sh, sc, wrt, rbias, cnt_in, tri, *, bs, tt, tail=None):
    bsz, seq, d = x.shape
    tm = bs * tt
    nt = seq // tt
    ntiles = (bsz // bs) * nt
    with_tail = tail is not None
    body = functools.partial(_out_router_body, bs=bs, tt=tt, with_tail=with_tail)
    tail_specs = [pl.BlockSpec((bs, TAIL_ROWS, d), lambda b, t: (b, 0, 0))] if with_tail else []
    tail_args = [tail] if with_tail else []
    return pl.pallas_call(
        body,
        grid=(bsz // bs, nt),
        in_specs=[
            pl.BlockSpec((bs, tt, d), lambda b, t: (b, t, 0)),
            pl.BlockSpec((d, d), lambda b, t: (0, 0), pipeline_mode=pl.Buffered(1)),
            pl.BlockSpec((bs, tt, d), lambda b, t: (b, t, 0)),
            pl.BlockSpec((bs, SUBLANES, d), lambda b, t: (b, 0, 0)),
            pl.BlockSpec((1, d), lambda b, t: (0, 0)),
            pl.BlockSpec((bs, SUBLANES, d), lambda b, t: (b, 0, 0)),
            pl.BlockSpec((bs, SUBLANES, d), lambda b, t: (b, 0, 0)),
            pl.BlockSpec((2 * ROUTER_ROWS, d), lambda b, t: (0, 0)),
            pl.BlockSpec((ROUTER_ROWS, 1), lambda b, t: (0, 0)),
            pl.BlockSpec((CLASS_ROWS, 128), lambda b, t: (0, 0)),
            pl.BlockSpec((tm, tm), lambda b, t: (0, 0)),
        ] + tail_specs,
        out_specs=[
            pl.BlockSpec((bs, tt, d), lambda b, t: (b, t, 0)),
            pl.BlockSpec((tm, ROW_WORDS), lambda b, t: (b * nt + t, 0)),
            pl.BlockSpec((1, 8, tm), lambda b, t: (b * nt + t, 0, 0)),
            pl.BlockSpec((CLASS_ROWS, 128), lambda b, t: (0, 0)),
        ],
        out_shape=[
            jax.ShapeDtypeStruct((bsz, seq, d), f32),
            jax.ShapeDtypeStruct((bsz * seq, ROW_WORDS), u32),
            jax.ShapeDtypeStruct((ntiles, 8, tm), i32),
            jax.ShapeDtypeStruct((CLASS_ROWS, 128), f32),
        ],
        scratch_shapes=[pltpu.VMEM((CLASS_ROWS, 128), f32)],
        compiler_params=_cparams(("arbitrary", "arbitrary")),
        name="out_router",
    )(a, w_bf, x, gt, gain, sh, sc, wrt, rbias, cnt_in, tri, *tail_args)


def _row_copy_scatter(src_ref, dst_ref, pos_ref, sem, r):
    return pltpu.make_async_copy(src_ref.at[pl.ds(r, 1), :], dst_ref.at[pl.ds(pos_ref[0, 0, r], 1), :], sem)


def _dispatch_body(pos_ref, rows_ref, dst_in_ref, dst_ref, sem, *, tm):
    del dst_in_ref

    def start(r, carry):
        _row_copy_scatter(rows_ref, dst_ref, pos_ref, sem, r).start()
        return carry

    lax.fori_loop(0, tm, start, 0, unroll=DMA_UNROLL)
    def wait(r, carry):
        _row_copy_scatter(rows_ref, dst_ref, pos_ref, sem, r).wait()
        return carry

    lax.fori_loop(0, tm, wait, 0, unroll=DMA_UNROLL)


def _dispatch(rows, pos3, sorted_rows, *, tm):
    n = rows.shape[0]
    body = functools.partial(_dispatch_body, tm=tm)
    return pl.pallas_call(
        body,
        grid=(n // tm,),
        in_specs=[
            pl.BlockSpec((1, 1, tm), lambda i: (i, 0, 0), memory_space=pltpu.SMEM),
            pl.BlockSpec((tm, ROW_WORDS), lambda i: (i, 0)),
            pl.BlockSpec(memory_space=pl.ANY),
        ],
        out_specs=pl.BlockSpec(memory_space=pl.ANY),
        out_shape=jax.ShapeDtypeStruct(sorted_rows.shape, sorted_rows.dtype),
        scratch_shapes=[pltpu.SemaphoreType.DMA(())],
        input_output_aliases={2: 0},
        compiler_params=_cparams(("arbitrary",)),
        name="moe_dispatch",
    )(pos3, rows, sorted_rows)


def _moe_body(ea_ref, eb_ref, valid_ref, xs_ref, w1a_ref, w3a_ref, w2a_ref, w1b_ref, w3b_ref, w2b_ref,
              ys_ref):
    del ea_ref, eb_ref
    i = pl.program_id(0)

    @pl.when(valid_ref[i] > 0)
    def _():
        left, right = _unpack_rows(xs_ref[:, :HALF])
        x = jnp.concatenate([left.astype(bf16), right.astype(bf16)], axis=1)
        gates = pltpu.bitcast(xs_ref[:, HALF:], f32)

        def expert(w1_ref, w3_ref, w2_ref, gate):
            h1 = _dot(x, w1_ref[0])
            h3 = _dot(x, w3_ref[0])
            hid = (h1 * _sigmoid(h1)) * h3 * gate
            return _dot(hid.astype(bf16), w2_ref[0])

        y = expert(w1a_ref, w3a_ref, w2a_ref, gates[:, 0:1]) + expert(w1b_ref, w3b_ref, w2b_ref, gates[:, 1:2])
        ys_ref[...] = _pack_rows(y)

    @pl.when(valid_ref[i] == 0)
    def _():
        ys_ref[...] = jnp.zeros(ys_ref.shape, u32)


def _moe(sorted_rows, tile_ea, tile_eb, tile_valid, w1_bf, w3_bf, w2_bf, *, tm):
    p = sorted_rows.shape[0]
    d, f = D_MODEL, D_EXPERT

    def wa(i, ea, eb, valid):
        return (ea[i], 0, 0)

    def wb(i, ea, eb, valid):
        return (eb[i], 0, 0)

    grid_spec = pltpu.PrefetchScalarGridSpec(
        num_scalar_prefetch=3,
        grid=(p // tm,),
        in_specs=[
            pl.BlockSpec((tm, ROW_WORDS), lambda i, ea, eb, valid: (i, 0)),
            pl.BlockSpec((1, d, f), wa), pl.BlockSpec((1, d, f), wa), pl.BlockSpec((1, f, d), wa),
            pl.BlockSpec((1, d, f), wb), pl.BlockSpec((1, d, f), wb), pl.BlockSpec((1, f, d), wb),
        ],
        out_specs=pl.BlockSpec((tm, HALF), lambda i, ea, eb, valid: (i, 0)),
    )
    return pl.pallas_call(
        _moe_body,
        grid_spec=grid_spec,
        out_shape=jax.ShapeDtypeStruct((p, HALF), u32),
        compiler_params=_cparams(("arbitrary",)),
        name="moe_experts",
    )(tile_ea, tile_eb, tile_valid, sorted_rows, w1_bf, w3_bf, w2_bf, w1_bf, w3_bf, w2_bf)


def _row_copy_gather(src_ref, dst_ref, pos_ref, sem, r):
    return pltpu.make_async_copy(src_ref.at[pl.ds(pos_ref[0, 0, r], 1), :], dst_ref.at[pl.ds(r, 1), :], sem)


def _combine_body(pos_ref, pos_next_ref, ys_ref, x_ref, gt_ref, fg_ref, o_ref, buf, sems, *, bs, tt, final_norm):
    tm = bs * tt
    nt = pl.num_programs(1)
    step = pl.program_id(0) * nt + pl.program_id(1)
    nsteps = pl.num_programs(0) * nt
    slot = step % 2

    def issue(p_ref, s):
        def start(r, carry):
            _row_copy_gather(ys_ref, buf.at[s], p_ref, sems.at[s], r).start()
            return carry
        lax.fori_loop(0, tm, start, 0, unroll=DMA_UNROLL)

    @pl.when(step == 0)
    def _():
        issue(pos_ref, 0)

    @pl.when(step + 1 < nsteps)
    def _():
        issue(pos_next_ref, 1 - slot)

    def wait(r, carry):
        _row_copy_gather(ys_ref, buf.at[slot], pos_ref, sems.at[slot], r).wait()
        return carry

    lax.fori_loop(0, tm, wait, 0, unroll=DMA_UNROLL)

    left, right = _unpack_rows(buf[slot])
    y = jnp.concatenate([left, right], axis=1).reshape(bs, tt, D_MODEL)
    xn = _gated_add(x_ref[...], gt_ref[...], y)
    if final_norm:
        x2 = xn.reshape(tm, D_MODEL)
        xn = (x2 * _inv_rms(x2) * fg_ref[...]).reshape(bs, tt, D_MODEL)
    o_ref[...] = xn


def _combine(ys, pos3, x, gt, final_g, *, bs, tt, final_norm):
    bsz, seq, d = x.shape
    tm = bs * tt
    nt = seq // tt
    nsteps = (bsz // bs) * nt
    body = functools.partial(_combine_body, bs=bs, tt=tt, final_norm=final_norm)
    return pl.pallas_call(
        body,
        grid=(bsz // bs, nt),
        in_specs=[
            pl.BlockSpec((1, 1, tm), lambda b, t: (b * nt + t, 0, 0), memory_space=pltpu.SMEM),
            pl.BlockSpec((1, 1, tm), lambda b, t: (jnp.minimum(b * nt + t + 1, nsteps - 1), 0, 0),
                         memory_space=pltpu.SMEM),
            pl.BlockSpec(memory_space=pl.ANY),
            pl.BlockSpec((bs, tt, d), lambda b, t: (b, t, 0)),
            pl.BlockSpec((bs, SUBLANES, d), lambda b, t: (b, 0, 0)),
            pl.BlockSpec((1, d), lambda b, t: (0, 0)),
        ],
        out_specs=pl.BlockSpec((bs, tt, d), lambda b, t: (b, t, 0)),
        out_shape=jax.ShapeDtypeStruct((bsz, seq, d), f32),
        scratch_shapes=[pltpu.VMEM((2, tm, HALF), u32), pltpu.SemaphoreType.DMA((2,))],
        compiler_params=_cparams(("arbitrary", "arbitrary")),
        name="moe_combine",
    )(pos3, pos3, ys, x, gt, final_g)


def _seg_sum(x, ones_bd):
    hi, lo = _split2(x)
    return _dot(hi, ones_bd) + _dot(lo, ones_bd)


def _rwkv_proj_body(x_ref, sh_ref, sc_ref, g_ref, mu_ref, shift_ref,
                    wr_ref, wk_ref, wv_ref, w1_ref, a1_ref, g1_ref, w2_ref, a2_ref, g2_ref,
                    vec_ref, ones_ref,
                    r_o, k_o, v_o, kk_o, b_o, g_o, lw_o, sh_o,
                    hs_scr, mix_scr, l1w_scr, l1a_scr, l1g_scr, *, bs, tt, tn):
    tm = bs * tt
    t = pl.program_id(1)
    j = pl.program_id(2)
    prev_row = CARRY_ROWS - 1

    @pl.when(j == 0)
    def _():
        @pl.when(t == 0)
        def _():
            hs_scr[:, prev_row:CARRY_ROWS, :] = shift_ref[...]

        h = _norm_mod(x_ref[...], g_ref[...], sh_ref[...], sc_ref[...])
        hs_scr[:, CARRY_ROWS:, :] = h
        h_prev = hs_scr[:, prev_row:prev_row + tt, :]
        h4 = h.reshape(bs, tt // SUBLANES, SUBLANES, D_MODEL)
        xx4 = h_prev.reshape(h4.shape) - h4
        for m in range(N_SHIFT_MIX):
            mix_scr[m] = (h4 + xx4 * mu_ref[m]).reshape(tm, D_MODEL).astype(bf16)
        last = hs_scr[:, prev_row + tt:CARRY_ROWS + tt, :]
        hs_scr[:, prev_row:CARRY_ROWS, :] = last
        sh_o[...] = last
        l1w_scr[...] = jnp.tanh(_dot(mix_scr[1], w1_ref[...])).astype(bf16)
        l1a_scr[...] = _dot(mix_scr[4], a1_ref[...]).astype(bf16)
        l1g_scr[...] = _sigmoid(_dot(mix_scr[5], g1_ref[...])).astype(bf16)

    vec = vec_ref[...]
    w0, a0, k_k, k_a = (vec[i * SUBLANES:(i + 1) * SUBLANES] for i in range(4))
    tiles = lambda x: x.reshape(tm // SUBLANES, SUBLANES, tn)
    r = _dot(mix_scr[0], wr_ref[...])
    k = tiles(_dot(mix_scr[2], wk_ref[...]))
    v = _dot(mix_scr[3], wv_ref[...])
    wl = w0 + tiles(_dot(l1w_scr[...], w2_ref[...]))
    a = _sigmoid(a0 + tiles(_dot(l1a_scr[...], a2_ref[...])))
    g = _dot(l1g_scr[...], g2_ref[...])
    neg = -wl
    softplus = jnp.maximum(neg, 0.0) + jnp.log(1.0 + jnp.exp(-jnp.abs(neg)))
    w_log = -softplus - 0.5
    lw = -jnp.exp(w_log)
    kkr = k * k_k
    ss = tiles(_dot((kkr * kkr).reshape(tm, tn).astype(bf16), ones_ref[...]))
    kk = kkr * lax.rsqrt(jnp.maximum(ss, 1e-24))
    k2 = k * (1.0 + (a - 1.0) * k_a)
    shp = (bs, tt, tn)
    r_o[...] = r.reshape(shp).astype(bf16)
    k_o[...] = k2.reshape(shp).astype(bf16)
    v_o[...] = v.reshape(shp).astype(bf16)
    kk_o[...] = kk.reshape(shp).astype(bf16)
    b_o[...] = (kk * a).reshape(shp).astype(bf16)
    g_o[...] = g.reshape(shp).astype(bf16)
    lw_o[...] = lw.reshape(shp)


def _rwkv_proj(x, sh, sc, gain, mu, shift, wr, wk, wv, w1p, a1p, g1, w2p, a2p, g2, vec, ones_bd,
               *, bs, tt, tn=256):
    bsz, seq, d = x.shape
    tm = bs * tt
    dg = g1.shape[1]
    body = functools.partial(_rwkv_proj_body, bs=bs, tt=tt, tn=tn)
    const2 = lambda b, t, j: (0, 0)
    colblk = lambda b, t, j: (0, j)
    tok = lambda b, t, j: (b, t, j)
    act = lambda dt: jax.ShapeDtypeStruct((bsz, seq, d), dt)
    return pl.pallas_call(
        body,
        grid=(bsz // bs, seq // tt, d // tn),
        in_specs=[
            pl.BlockSpec((bs, tt, d), lambda b, t, j: (b, t, 0)),
            pl.BlockSpec((bs, SUBLANES, d), lambda b, t, j: (b, 0, 0)),
            pl.BlockSpec((bs, SUBLANES, d), lambda b, t, j: (b, 0, 0)),
            pl.BlockSpec((1, d), const2),
            pl.BlockSpec((N_SHIFT_MIX, SUBLANES, d), lambda b, t, j: (0, 0, 0)),
            pl.BlockSpec((bs, 1, d), lambda b, t, j: (b, 0, 0)),
            pl.BlockSpec((d, tn), colblk), pl.BlockSpec((d, tn), colblk), pl.BlockSpec((d, tn), colblk),
            pl.BlockSpec((d, LORA_PAD), const2), pl.BlockSpec((d, LORA_PAD), const2),
            pl.BlockSpec((d, dg), const2),
            pl.BlockSpec((LORA_PAD, tn), colblk), pl.BlockSpec((LORA_PAD, tn), colblk),
            pl.BlockSpec((dg, tn), colblk),
            pl.BlockSpec((4 * SUBLANES, tn), colblk),
            pl.BlockSpec((tn, tn), const2),
        ],
        out_specs=[pl.BlockSpec((bs, tt, tn), tok)] * 7 + [pl.BlockSpec((bs, 1, d), lambda b, t, j: (b, 0, 0))],
        out_shape=[act(bf16)] * 6 + [act(f32), jax.ShapeDtypeStruct((bsz, 1, d), f32)],
        scratch_shapes=[
            pltpu.VMEM((bs, tt + CARRY_ROWS, d), f32),
            pltpu.VMEM((N_SHIFT_MIX, tm, d), bf16),
            pltpu.VMEM((tm, LORA_PAD), bf16),
            pltpu.VMEM((tm, LORA_PAD), bf16),
            pltpu.VMEM((tm, dg), bf16),
        ],
        compiler_params=_cparams(("arbitrary", "arbitrary", "arbitrary")),
        name="rwkv_proj",
    )(x, sh, sc, gain, mu, shift, wr, wk, wv, w1p, a1p, g1, w2p, a2p, g2, vec, ones_bd)


def _block_diag_rows(x, heads):
    return jnp.concatenate([jnp.where(m, x, 0.0) for m in heads], axis=1).astype(bf16)


def _bmm(a, b):
    return lax.dot_general(a, b, (((2,), (1,)), ((0,), (0,))), preferred_element_type=f32)


def _bmm_nt(a, b):
    return lax.dot_general(a, b, (((2,), (2,)), ((0,), (0,))), preferred_element_type=f32)


def _wkv_chunk(r, k, v, kk, beta, lw, c, c_end, state, masks):
    heads, strict, incl, eye, bd_mask = masks
    n = WKV_CHUNK
    bd = lambda x: _block_diag_rows(x, heads)
    g_inv = jnp.exp(-c)
    g_end = jnp.exp(c_end - c)
    a_t = -(kk * jnp.exp(c - lw))
    r_t = r * jnp.exp(c)
    ar = jnp.concatenate([a_t, r_t], axis=1).astype(bf16)
    a1 = _bmm_nt(ar, bd(beta * g_inv))
    a2 = _bmm_nt(ar, bd(k * g_inv))
    a_ab = jnp.where(strict, a1[:, :n], 0.0)
    a_rb = jnp.where(incl, a1[:, n:], 0.0)
    a_ak = jnp.where(strict, a2[:, :n], 0.0)
    a_rk = jnp.where(incl, a2[:, n:], 0.0)
    v_bd = bd(v)
    av = _bmm(a_ak.astype(bf16), v_bd)
    ov = _bmm(a_rk.astype(bf16), v_bd)
    q = a_ab
    p = eye + q
    q = _bmm(q.astype(bf16), bd(q))
    for level in range(5):
        if level < 4:
            res = _bmm(jnp.concatenate([p, q], axis=1).astype(bf16), bd(q))
            p = p + res[:, :n]
            q = res[:, n:]
        else:
            p = p + _bmm(p.astype(bf16), bd(q))
    p_bf = p.astype(bf16)
    a_bar = _bmm(p_bf, bd(a_t))
    u0 = _bmm(p_bf, bd(av))
    s_bf = state.astype(bf16)
    from_state = _bmm_nt(jnp.concatenate([a_bar, r_t], axis=1).astype(bf16), s_bf)
    u = from_state[:, :n] + u0
    o = from_state[:, n:] + ov + _bmm(a_rb.astype(bf16), bd(u))
    uv_t = jnp.swapaxes(jnp.concatenate([u, v], axis=1), 1, 2).astype(bf16)
    bk = jnp.concatenate([beta * g_end, k * g_end], axis=1).astype(bf16)
    decay = jnp.exp(c_end)
    new_state = (state * jnp.concatenate([decay] * WKV_HEADS, axis=1)
                 + jnp.where(bd_mask, _bmm(uv_t, bk), 0.0))
    return o, new_state


def _wkv_body(r_ref, k_ref, v_ref, kk_ref, b_ref, g_ref, lw_ref, s0_ref, vec_ref, tril_ref, ones_ref,
              z_ref, so_ref, s_scr, *, tc):
    t = pl.program_id(1)
    n = WKV_CHUNK

    @pl.when(t == 0)
    def _():
        s_scr[...] = s0_ref[0]

    lane = lax.broadcasted_iota(i32, (n, PAIR), 1)
    row = lax.broadcasted_iota(i32, (n, PAIR), 0)
    head_of = lambda idx: jnp.right_shift(idx, HEAD_SIZE.bit_length() - 1)
    heads = [head_of(lane) == h for h in range(WKV_HEADS)]
    within = lane & (HEAD_SIZE - 1)
    strict = within < row
    incl = within <= row
    eye = (within == row).astype(f32)
    bd_mask = (head_of(lax.broadcasted_iota(i32, (PAIR, PAIR), 0))
               == head_of(lax.broadcasted_iota(i32, (PAIR, PAIR), 1)))
    masks = (heads, strict, incl, eye, bd_mask)
    tril = tril_ref[...]
    ones_bd = ones_ref[...]
    rows = min(tc, n)
    nchunks = max(tc // n, 1)

    def units(x):
        return jnp.stack([x[:, p * PAIR:(p + 1) * PAIR] for p in range(N_PAIRS)], axis=0)

    def seg_sum(x):
        flat = x.reshape(N_PAIRS * n, PAIR).astype(bf16)
        return _dot(flat, ones_bd).reshape(N_PAIRS, n, PAIR)

    vec = units(vec_ref[...])
    ln_w, ln_b, r_k = (vec[:, i * SUBLANES:(i + 1) * SUBLANES] for i in range(3))

    def load(ref, ci):
        x = ref[0, ci * n:ci * n + rows, :].astype(f32)
        if rows < n:
            x = jnp.concatenate([x, jnp.zeros((n - rows, x.shape[1]), f32)], axis=0)
        return x

    for ci in range(nchunks):
        lw = load(lw_ref, ci)
        hi = lw.astype(bf16)
        rem = lw - hi.astype(f32)
        mid = rem.astype(bf16)
        lo = (rem - mid.astype(f32)).astype(bf16)
        sums = _dot(tril, hi) + _dot(tril, mid) + _dot(tril, lo)
        r, k, v = (units(load(ref, ci)) for ref in (r_ref, k_ref, v_ref))
        o, s_new = _wkv_chunk(r, k, v, units(load(kk_ref, ci)), units(load(b_ref, ci)), units(lw),
                              units(sums[:n]), units(sums[n:]), s_scr[...], masks)
        s_scr[...] = s_new
        mean = seg_sum(o) * (1.0 / HEAD_SIZE)
        dev = o - mean
        var = seg_sum(dev * dev) * (1.0 / HEAD_SIZE)
        y = _per_seq(dev * lax.rsqrt(var + GN_EPS), lambda y4, w4, b4: y4 * w4 + b4, ln_w, ln_b)
        y = y + seg_sum(_per_seq(r * k, lambda x4, rk4: x4 * rk4, r_k)) * v
        z = (y * units(load(g_ref, ci))).astype(bf16)
        z_ref[0, ci * n:ci * n + rows, :] = jnp.concatenate([z[p, :rows] for p in range(N_PAIRS)], axis=1)

    @pl.when(t == pl.num_programs(1) - 1)
    def _():
        so_ref[0] = s_scr[...]


def _wkv(r, k, v, kk, beta, g, lw, state_bd, vec, tril, ones_bd, *, tc):
    bsz, seq, d = r.shape
    body = functools.partial(_wkv_body, tc=tc)
    tok = pl.BlockSpec((1, tc, d), lambda b, t: (b, t, 0))
    st = pl.BlockSpec((1, N_PAIRS, PAIR, PAIR), lambda b, t: (b, 0, 0, 0))
    return pl.pallas_call(
        body,
        grid=(bsz, seq // tc),
        in_specs=[tok] * 7 + [
            st,
            pl.BlockSpec((3 * SUBLANES, d), lambda b, t: (0, 0)),
            pl.BlockSpec((2 * WKV_CHUNK, WKV_CHUNK), lambda b, t: (0, 0)),
            pl.BlockSpec((PAIR, PAIR), lambda b, t: (0, 0)),
        ],
        out_specs=[tok, st],
        out_shape=[jax.ShapeDtypeStruct((bsz, seq, d), bf16),
                   jax.ShapeDtypeStruct(state_bd.shape, f32)],
        scratch_shapes=[pltpu.VMEM((N_PAIRS, PAIR, PAIR), f32)],
        compiler_params=_cparams(("arbitrary", "arbitrary")),
        name="wkv",
    )(r, k, v, kk, beta, g, lw, state_bd, vec, tril, ones_bd)


def _block_ones(n):
    idx = jnp.arange(n) // HEAD_SIZE
    return (idx[:, None] == idx[None, :]).astype(bf16)


def _state_to_pairs(s):
    b = s.shape[0]
    s5 = s.reshape(b, N_PAIRS, WKV_HEADS, HEAD_SIZE, HEAD_SIZE)
    zero = jnp.zeros((b, N_PAIRS, HEAD_SIZE, HEAD_SIZE), s.dtype)
    rows = [jnp.concatenate([s5[:, :, h] if j == h else zero for j in range(WKV_HEADS)], axis=-1)
            for h in range(WKV_HEADS)]
    return jnp.concatenate(rows, axis=-2)


def _pairs_to_state(bd):
    b = bd.shape[0]
    blocks = [bd[:, :, h * HEAD_SIZE:(h + 1) * HEAD_SIZE, h * HEAD_SIZE:(h + 1) * HEAD_SIZE]
              for h in range(WKV_HEADS)]
    return jnp.stack(blocks, axis=2).reshape(b, N_HEADS, HEAD_SIZE, HEAD_SIZE)


_PAIR_TABLE = ((0, 1), (0, 2), (0, 3), (1, 2), (1, 3), (2, 3))


def _moe_plan(counts, n_tiles, tm):
    cnt = counts[:N_CLASSES].astype(i32)
    tiles_per = (cnt + tm - 1) // tm
    tile_end = jnp.cumsum(tiles_per)
    tile_start = tile_end - tiles_per
    offsets = tile_start * tm
    tidx = jnp.arange(n_tiles, dtype=i32)
    cls_of_tile = jnp.sum((tidx[:, None] >= tile_end[None, :]).astype(i32), axis=1)
    valid = (cls_of_tile < N_CLASSES).astype(i32)
    last_cls = jnp.max(jnp.where(tiles_per > 0, jnp.arange(N_CLASSES, dtype=i32), 0))
    cls_c = jnp.where(valid > 0, cls_of_tile, last_cls)
    grp = cls_c // 6
    pr = cls_c % 6
    lo_tab = jnp.array([p[0] for p in _PAIR_TABLE], i32)
    hi_tab = jnp.array([p[1] for p in _PAIR_TABLE], i32)
    ea = grp * EXPERTS_PER_GROUP + lo_tab[pr]
    eb = grp * EXPERTS_PER_GROUP + hi_tab[pr]
    return offsets, ea, eb, valid


def _positions(info, offsets):
    cls = info[:, 0, :].reshape(-1)
    rank = info[:, 1, :].reshape(-1)
    return offsets[cls] + rank


def _router_weights(wg, bg, we, be):
    w = jnp.concatenate([wg, we], axis=1).T
    w = jnp.pad(w, ((0, ROUTER_ROWS - w.shape[0]), (0, 0)))
    hi, lo = _split2_outside(w)
    bias = jnp.pad(jnp.concatenate([bg, be]), (0, ROUTER_ROWS - N_GROUPS - N_EXPERTS))
    return jnp.concatenate([hi, lo], axis=0), bias.reshape(ROUTER_ROWS, 1)


def _tiles(bsz, seq, rows):
    tt = min(seq, rows)
    bs = max(rows // tt, 1)
    assert bsz % bs == 0 and seq % tt == 0
    return bs, tt


def kernel(x_prompt, x_sample, c_prompt, c_sample, state_conv, state_shift, state_wkv, ada_w, ada_b, norm_g, final_g, cv_in, cv_w, cv_out, rw_mu, rw_r, rw_k, rw_v, rw_o, rw_w0, rw_w1, rw_w2, rw_a0, rw_a1, rw_a2, rw_g1, rw_g2, rw_kk, rw_ka, rw_rk, rw_lnw, rw_lnb, moe_wg, moe_bg, moe_we, moe_be, moe_w1, moe_w3, moe_w2):
    d = D_MODEL
    xs = [x_prompt, x_sample]
    bszs = [x.shape[0] for x in xs]
    seqs = [x.shape[1] for x in xs]
    ntok = [b * s for b, s in zip(bszs, seqs)]
    n_total = sum(ntok)

    c_all = jnp.concatenate([c_prompt, c_sample], axis=0)
    rows_c = c_all.shape[0]
    rows_pad = -(-rows_c // 8) * 8
    mod = _ada(jnp.pad(c_all, ((0, rows_pad - rows_c), (0, 0))), ada_w, ada_b)

    def mods(layer, trunk):
        lo = 0 if trunk == 0 else bszs[0]
        m = mod[layer, lo:lo + bszs[trunk]]
        return [jnp.broadcast_to(m[:, None, i * d:(i + 1) * d], (bszs[trunk], SUBLANES, d))
                for i in range(N_ADA)]

    conv_states = [jnp.zeros((1, bszs[0], CONV_WIDTH - 1, d), f32), state_conv]
    shift_states = [jnp.zeros((1, bszs[0], d), f32), state_shift]
    wkv_states = [jnp.zeros((1, bszs[0], N_HEADS, HEAD_SIZE, HEAD_SIZE), f32), state_wkv]

    tri_cache = {}

    def tri(n):
        if n not in tri_cache:
            tri_cache[n] = (jnp.arange(n)[:, None] < jnp.arange(n)[None, :]).astype(bf16)
        return tri_cache[n]

    n_tiles = n_total // MOE_TILE + N_CLASSES
    p_rows = n_tiles * MOE_TILE
    expert_w = [w.reshape((-1,) + w.shape[2:]).astype(bf16) for w in (moe_w1, moe_w3, moe_w2)]

    def moe_layer(layer, rows_list, info_list, counts):
        offsets, ea, eb, valid = _moe_plan(counts[:, 0], n_tiles, MOE_TILE)
        pos_list = [_positions(info, offsets) for info in info_list]
        sorted_rows = jnp.zeros((p_rows, ROW_WORDS), u32)
        for rows, pos in zip(rows_list, pos_list):
            sorted_rows = _dispatch(rows, pos.reshape(-1, 1, MOE_TILE), sorted_rows, tm=MOE_TILE)
        first = layer * N_EXPERTS
        ys = _moe(sorted_rows, ea + first, eb + first, valid, *expert_w, tm=MOE_TILE)
        return ys, pos_list

    def out_router_pair(layer, acts, w_bf, xcur, tails=(None, None)):
        wrt, rbias = _router_weights(moe_wg[layer], moe_bg[layer], moe_we[layer], moe_be[layer])
        counts = jnp.zeros((CLASS_ROWS, 128), f32)
        x_new, rows_list, info_list = [], [], []
        for trunk in range(2):
            _, _, gt_m, sh_f, sc_f, _ = mods(layer, trunk)
            bs, tt = _tiles(bszs[trunk], seqs[trunk], ROUTER_TILE)
            xo, rows, info, counts = _out_router(
                acts[trunk], w_bf, xcur[trunk], gt_m, norm_g[layer, 1].reshape(1, d), sh_f, sc_f,
                wrt, rbias, counts, tri(bs * tt), bs=bs, tt=tt, tail=tails[trunk])
            x_new.append(xo)
            rows_list.append(rows)
            info_list.append(info)
        return x_new, rows_list, info_list, counts

    def combine_pair(layer, ys, pos_list, xcur, final):
        out = []
        for trunk in range(2):
            gt_f = mods(layer, trunk)[5]
            bs, tt = _tiles(bszs[trunk], seqs[trunk], COMBINE_TILE)
            out.append(_combine(ys, pos_list[trunk].reshape(-1, 1, bs * tt), xcur[trunk], gt_f,
                                final_g.reshape(1, d), bs=bs, tt=tt, final_norm=final))
        return out

    w_in_hi, w_in_lo = _split2_outside(cv_in[0])
    w_out_hi, w_out_lo = _split2_outside(cv_out[0])
    gain0 = norm_g[0, 0].reshape(1, d)
    conv_taps = jnp.repeat(cv_w[0], SUBLANES, axis=0)
    z0, conv_out = [], []
    for trunk in range(2):
        sh_m, sc_m = mods(0, trunk)[:2]
        bs, tt = _tiles(bszs[trunk], seqs[trunk], PROJ_TILE)
        st8 = jnp.pad(conv_states[trunk][0], ((0, 0), (CARRY_ROWS - (CONV_WIDTH - 1), 0), (0, 0)))
        z, so = _conv_in(xs[trunk], sh_m, sc_m, gain0, cv_in[0].astype(bf16), conv_taps, st8, bs=bs, tt=tt)
        z0.append(z)
        conv_out.append(so[:, -1, CARRY_ROWS - (CONV_WIDTH - 1):, :][None])
    assert min(seqs) >= TAIL_WINDOW
    x_tail = jnp.concatenate([x[:, -TAIL_WINDOW:, :] for x in xs], axis=0)
    n_seq = x_tail.shape[0]
    mod_tail = [jnp.concatenate([mods(0, trunk)[i] for trunk in range(2)], axis=0) for i in range(3)]
    z_tail, _ = _conv_in(x_tail, mod_tail[0], mod_tail[1], gain0, w_in_hi, conv_taps,
                         jnp.zeros((n_seq, CARRY_ROWS, d), f32), bs=n_seq, tt=TAIL_WINDOW, w_in_lo=w_in_lo)
    x1_tail = _tail_out(z_tail, x_tail, mod_tail[2], w_out_hi, w_out_lo)[:, TAIL_WINDOW - TAIL_ROWS:, :]
    tails = (x1_tail[:bszs[0]], x1_tail[bszs[0]:])
    x1, rows0, info0, counts0 = out_router_pair(0, z0, cv_out[0].astype(bf16), xs, tails)
    ys0, pos0 = moe_layer(0, rows0, info0, counts0)
    x2 = combine_pair(0, ys0, pos0, x1, False)

    pad_l = ((0, 0), (0, LORA_PAD - rw_w1.shape[2]))
    pad_r = ((0, LORA_PAD - rw_w2.shape[1]), (0, 0))
    w1p = jnp.pad(rw_w1[0], pad_l).astype(bf16)
    a1p = jnp.pad(rw_a1[0], pad_l).astype(bf16)
    w2p = jnp.pad(rw_w2[0], pad_r).astype(bf16)
    a2p = jnp.pad(rw_a2[0], pad_r).astype(bf16)
    vec_p = jnp.repeat(jnp.stack([rw_w0[0], rw_a0[0], rw_kk[0], rw_ka[0]]), SUBLANES, axis=0)
    vec_w = jnp.repeat(jnp.stack([rw_lnw[0], rw_lnb[0], rw_rk[0].reshape(d)]), SUBLANES, axis=0)
    wr_bf, wk_bf, wv_bf = rw_r[0].astype(bf16), rw_k[0].astype(bf16), rw_v[0].astype(bf16)
    g1_bf, g2_bf = rw_g1[0].astype(bf16), rw_g2[0].astype(bf16)
    proj_tn = 256
    tril = jnp.concatenate([jnp.arange(WKV_CHUNK)[:, None] >= jnp.arange(WKV_CHUNK)[None, :],
                            jnp.ones((WKV_CHUNK, WKV_CHUNK), bool)], axis=0).astype(bf16)
    z1, shift_out, wkv_out = [], [], []
    for trunk in range(2):
        sh_m, sc_m = mods(1, trunk)[:2]
        bs, tt = _tiles(bszs[trunk], seqs[trunk], PROJ_TILE)
        r, k, v, kk, beta, g, lw, sho = _rwkv_proj(
            x2[trunk], sh_m, sc_m, norm_g[1, 0].reshape(1, d),
            jnp.broadcast_to(rw_mu[0][:, None, :], (N_SHIFT_MIX, SUBLANES, d)),
            shift_states[trunk][0].reshape(bszs[trunk], 1, d), wr_bf, wk_bf, wv_bf, w1p, a1p, g1_bf,
            w2p, a2p, g2_bf, vec_p, _block_ones(proj_tn), bs=bs, tt=tt, tn=proj_tn)
        tc = min(seqs[trunk], 2 * WKV_CHUNK)
        z, s_bd = _wkv(r, k, v, kk, beta, g, lw, _state_to_pairs(wkv_states[trunk][0]), vec_w, tril,
                       _block_ones(PAIR), tc=tc)
        z1.append(z)
        shift_out.append(sho.reshape(1, bszs[trunk], d))
        wkv_out.append(_pairs_to_state(s_bd)[None])
    x3, rows1, info1, counts1 = out_router_pair(1, z1, rw_o[0].astype(bf16), x2)
    ys1, pos1 = moe_layer(1, rows1, info1, counts1)
    y = combine_pair(1, ys1, pos1, x3, True)

    return (y[0], y[1], conv_out[0], shift_out[0], wkv_out[0], conv_out[1], shift_out[1], wkv_out[1])
```

```python
import functools

import jax
import jax.numpy as jnp
from jax import lax
from jax.experimental import pallas as pl
from jax.experimental.pallas import tpu as pltpu

f32 = jnp.float32
bf16 = jnp.bfloat16
i32 = jnp.int32
u32 = jnp.uint32

D_MODEL = 2048
LANES = 128
SUBLANES = 8
HEAD_SIZE = 64
N_HEADS = D_MODEL // HEAD_SIZE
WKV_HEADS = 2
PAIR = WKV_HEADS * HEAD_SIZE
N_PAIRS = D_MODEL // PAIR
CONV_WIDTH = 3
N_GROUPS = 4
EXPERTS_PER_GROUP = 4
N_EXPERTS = N_GROUPS * EXPERTS_PER_GROUP
D_EXPERT = D_MODEL // 4
N_ADA = 6
N_SHIFT_MIX = 6
RMS_EPS = 1e-6
GN_EPS = 64e-5
LORA_PAD = 128
N_CLASSES = N_GROUPS * 6
CLASS_ROWS = 32
ROUTER_ROWS = 128
HALF = D_MODEL // 2
ROW_WORDS = HALF + 128
WKV_CHUNK = 64
WKV_CHUNKS_PER_STEP = 4
CARRY_ROWS = 8
TAIL_ROWS = 8
TAIL_WINDOW = 2 * TAIL_ROWS

V7X_VMEM_LIMIT = 56 * 1024 * 1024
MOE_TILE = 512
DMA_UNROLL = 8
ROUTER_TILE = 512
ROUTER_SPLIT = 1
COMBINE_TILE = 512
PROJ_TILE = 512


def _cparams(sem, vmem=V7X_VMEM_LIMIT):
    return pltpu.CompilerParams(dimension_semantics=sem, vmem_limit_bytes=vmem)


def _dot(a, b):
    return jnp.dot(a, b, preferred_element_type=f32)


def _dot_nt(a, b):
    return lax.dot_general(a, b, (((1,), (1,)), ((), ())), preferred_element_type=f32)


def _dot_tn(a, b):
    return lax.dot_general(a, b, (((0,), (0,)), ((), ())), preferred_element_type=f32)


def _split2(x):
    hi = pltpu.bitcast(pltpu.bitcast(x, u32) & jnp.uint32(0xFFFF0000), f32)
    return hi.astype(bf16), (x - hi).astype(bf16)


def _split2_outside(x):
    hi = lax.bitcast_convert_type(lax.bitcast_convert_type(x, u32) & jnp.uint32(0xFFFF0000), f32)
    return hi.astype(bf16), (x - hi).astype(bf16)


def _sigmoid(x):
    return 1.0 / (1.0 + jnp.exp(-x))


def _pack_rows(x):
    hi = pltpu.bitcast(x[:, :HALF].astype(bf16).astype(f32), u32)
    lo = pltpu.bitcast(x[:, HALF:].astype(bf16).astype(f32), u32)
    return (hi & jnp.uint32(0xFFFF0000)) | (lo >> 16)


def _unpack_rows(p):
    left = pltpu.bitcast(p & jnp.uint32(0xFFFF0000), f32)
    right = pltpu.bitcast(p << 16, f32)
    return left, right


def _dot3(a_hi, a_lo, w_hi, w_lo):
    return _dot(a_hi, w_hi) + _dot(a_lo, w_hi) + _dot(a_hi, w_lo)


def _ada_body(c_ref, w_ref, b_ref, o_ref):
    c = c_ref[...]
    s_hi, s_lo = _split2(c * _sigmoid(c))
    w_hi, w_lo = _split2(w_ref[0])
    o_ref[0] = _dot3(s_hi, s_lo, w_hi, w_lo) + b_ref[0]


def _ada(c_all, ada_w, ada_b, tn=1024):
    depth, d, n = ada_w.shape
    rows = c_all.shape[0]
    return pl.pallas_call(
        _ada_body,
        grid=(depth, n // tn),
        in_specs=[
            pl.BlockSpec((rows, d), lambda l, j: (0, 0)),
            pl.BlockSpec((1, d, tn), lambda l, j: (l, 0, j)),
            pl.BlockSpec((1, 1, tn), lambda l, j: (l, 0, j)),
        ],
        out_specs=pl.BlockSpec((1, rows, tn), lambda l, j: (l, 0, j)),
        out_shape=jax.ShapeDtypeStruct((depth, rows, n), f32),
        compiler_params=_cparams(("arbitrary", "arbitrary")),
        name="ada",
    )(c_all, ada_w, ada_b.reshape(depth, 1, n))


def _inv_rms(x2):
    sq = x2 * x2
    part = sq[:, 0:LANES]
    for i in range(1, D_MODEL // LANES):
        part = part + sq[:, i * LANES:(i + 1) * LANES]
    hi, lo = _split2(part)
    ones = jnp.ones((LANES, LANES), bf16)
    ms = (_dot(hi, ones) + _dot(lo, ones)) * (1.0 / D_MODEL)
    rs = lax.rsqrt(ms + RMS_EPS)
    return jnp.concatenate([rs] * (D_MODEL // LANES), axis=1)


def _per_seq(x, fn, *vecs):
    bs, tt, n = x.shape
    x4 = x.reshape(bs, tt // SUBLANES, SUBLANES, n)
    return fn(x4, *(v[:, None] for v in vecs)).reshape(bs, tt, n)


def _gated_add(x, gate, y):
    return _per_seq(y, lambda y4, g4: g4 * y4, gate) + x


def _norm_mod(x, gain, sh, sc):
    bs, tt, d = x.shape
    x2 = x.reshape(bs * tt, d)
    y = (x2 * _inv_rms(x2) * gain).reshape(bs, tt, d)
    return _per_seq(y, lambda y4, sh4, sc4: y4 * (1.0 + sc4) + sh4, sh, sc)


def _conv_in_body(x_ref, sh_ref, sc_ref, g_ref, cw_ref, st_ref, *rest, bs, tt, tn, precise):
    nw = 6 if precise else 3
    w_refs = rest[:nw]
    z_ref, so_ref, h_scr, carry_scr, uext_scr = rest[nw:]
    t = pl.program_id(1)
    j = pl.program_id(2)

    @pl.when(j == 0)
    def _():
        h = _norm_mod(x_ref[...], g_ref[...], sh_ref[...], sc_ref[...]).reshape(bs * tt, D_MODEL)
        if precise:
            h_scr[0], h_scr[1] = _split2(h)
        else:
            h_scr[0] = h.astype(bf16)

    @pl.when(t == 0)
    def _():
        carry_scr[j] = st_ref[...]

    if precise:
        b_gate, c_gate, xv = (_dot3(h_scr[0], h_scr[1], w_refs[i][...], w_refs[i + 3][...]) for i in range(3))
    else:
        b_gate, c_gate, xv = (_dot(h_scr[0], w_refs[i][...]) for i in range(3))
    u = (c_gate * xv).reshape(bs, tt, tn)
    uext_scr[:, 0:CARRY_ROWS, :] = carry_scr[j]
    uext_scr[:, CARRY_ROWS:, :] = u
    cw = cw_ref[...]
    tap = lambda i: cw[i * SUBLANES:(i + 1) * SUBLANES]
    tiles = lambda x: x.reshape(bs, tt // SUBLANES, SUBLANES, tn)
    conv = (tiles(uext_scr[:, CARRY_ROWS - 2:CARRY_ROWS - 2 + tt, :]) * tap(0)
            + tiles(uext_scr[:, CARRY_ROWS - 1:CARRY_ROWS - 1 + tt, :]) * tap(1)
            + tiles(u) * tap(2))
    z_ref[...] = (b_gate.reshape(bs, tt, tn) * conv.reshape(bs, tt, tn)).astype(z_ref.dtype)
    last = uext_scr[:, tt:tt + CARRY_ROWS, :]
    carry_scr[j] = last
    so_ref[:, 0] = last


def _conv_in(x, sh, sc, gain, w_in_hi, conv_w, state8, *, bs, tt, tn=512, w_in_lo=None):
    bsz, seq, d = x.shape
    nj = d // tn
    precise = w_in_lo is not None
    body = functools.partial(_conv_in_body, bs=bs, tt=tt, tn=tn, precise=precise)
    w_specs = [pl.BlockSpec((d, tn), lambda b, t, j, k=k: (0, j + k * nj)) for k in range(3)]
    weights = [w_in_hi] * 3 + ([w_in_lo] * 3 if precise else [])
    return pl.pallas_call(
        body,
        grid=(bsz // bs, seq // tt, nj),
        in_specs=[
            pl.BlockSpec((bs, tt, d), lambda b, t, j: (b, t, 0)),
            pl.BlockSpec((bs, SUBLANES, d), lambda b, t, j: (b, 0, 0)),
            pl.BlockSpec((bs, SUBLANES, d), lambda b, t, j: (b, 0, 0)),
            pl.BlockSpec((1, d), lambda b, t, j: (0, 0)),
            pl.BlockSpec((CONV_WIDTH * SUBLANES, tn), lambda b, t, j: (0, j)),
            pl.BlockSpec((bs, CARRY_ROWS, tn), lambda b, t, j: (b, 0, j)),
        ] + w_specs * (2 if precise else 1),
        out_specs=[
            pl.BlockSpec((bs, tt, tn), lambda b, t, j: (b, t, j)),
            pl.BlockSpec((bs, 1, CARRY_ROWS, tn), lambda b, t, j: (b, t, 0, j)),
        ],
        out_shape=[
            jax.ShapeDtypeStruct((bsz, seq, d), f32 if precise else bf16),
            jax.ShapeDtypeStruct((bsz, seq // tt, CARRY_ROWS, d), f32),
        ],
        scratch_shapes=[
            pltpu.VMEM((2 if precise else 1, bs * tt, d), bf16),
            pltpu.VMEM((nj, bs, CARRY_ROWS, tn), f32),
            pltpu.VMEM((bs, tt + CARRY_ROWS, tn), f32),
        ],
        compiler_params=_cparams(("arbitrary", "arbitrary", "arbitrary")),
        name="conv_in_precise" if precise else "conv_in",
    )(x, sh, sc, gain, conv_w, state8, *weights)


def _tail_out_body(z_ref, x_ref, gt_ref, wh_ref, wl_ref, o_ref, *, bs, tt):
    z_hi, z_lo = _split2(z_ref[...].reshape(bs * tt, D_MODEL))
    y = _dot3(z_hi, z_lo, wh_ref[...], wl_ref[...])
    tn = y.shape[1]
    o_ref[...] = _gated_add(x_ref[...], gt_ref[...], y.reshape(bs, tt, tn))


def _tail_out(z, x, gt, w_hi, w_lo, *, tn=512):
    bsz, tt, d = x.shape
    body = functools.partial(_tail_out_body, bs=bsz, tt=tt)
    return pl.pallas_call(
        body,
        grid=(d // tn,),
        in_specs=[
            pl.BlockSpec((bsz, tt, d), lambda j: (0, 0, 0)),
            pl.BlockSpec((bsz, tt, tn), lambda j: (0, 0, j)),
            pl.BlockSpec((bsz, SUBLANES, tn), lambda j: (0, 0, j)),
            pl.BlockSpec((d, tn), lambda j: (0, j)),
            pl.BlockSpec((d, tn), lambda j: (0, j)),
        ],
        out_specs=pl.BlockSpec((bsz, tt, tn), lambda j: (0, 0, j)),
        out_shape=jax.ShapeDtypeStruct((bsz, tt, d), f32),
        compiler_params=_cparams(("arbitrary",)),
        name="tail_out",
    )(z, x, gt, w_hi, w_lo)


def _route_rows(logit):
    lg = [logit[g:g + 1, :] for g in range(N_GROUPS)]
    le = [logit[N_GROUPS + e:N_GROUPS + e + 1, :] for e in range(N_EXPERTS)]
    gmax = jnp.maximum(jnp.maximum(lg[0], lg[1]), jnp.maximum(lg[2], lg[3]))
    gidx = jnp.where(lg[0] == gmax, 0, jnp.where(lg[1] == gmax, 1, jnp.where(lg[2] == gmax, 2, 3)))
    denom = (jnp.exp(lg[0] - gmax) + jnp.exp(lg[1] - gmax)
             + jnp.exp(lg[2] - gmax) + jnp.exp(lg[3] - gmax))
    p_grp = 1.0 / denom
    leg = [jnp.where(gidx == 0, le[i],
                     jnp.where(gidx == 1, le[EXPERTS_PER_GROUP + i],
                               jnp.where(gidx == 2, le[2 * EXPERTS_PER_GROUP + i],
                                         le[3 * EXPERTS_PER_GROUP + i])))
           for i in range(EXPERTS_PER_GROUP)]
    v1 = jnp.maximum(jnp.maximum(leg[0], leg[1]), jnp.maximum(leg[2], leg[3]))
    i1 = jnp.where(leg[0] == v1, 0, jnp.where(leg[1] == v1, 1, jnp.where(leg[2] == v1, 2, 3)))
    neg = jnp.float32(-jnp.inf)
    rest = [jnp.where(i1 == i, neg, leg[i]) for i in range(EXPERTS_PER_GROUP)]
    v2 = jnp.maximum(jnp.maximum(rest[0], rest[1]), jnp.maximum(rest[2], rest[3]))
    i2 = jnp.where((rest[0] == v2) & (i1 != 0), 0,
                   jnp.where((rest[1] == v2) & (i1 != 1), 1,
                             jnp.where((rest[2] == v2) & (i1 != 2), 2, 3)))
    s = jnp.exp(v2 - v1)
    w_first = p_grp / (1.0 + s)
    w_second = p_grp * s / (1.0 + s)
    i_lo = jnp.minimum(i1, i2)
    i_hi = jnp.maximum(i1, i2)
    pair_base = jnp.where(i_lo == 0, 0, jnp.where(i_lo == 1, 3, 5))
    cls = gidx * 6 + pair_base + (i_hi - i_lo - 1)
    g_lo = jnp.where(i1 < i2, w_first, w_second)
    g_hi = jnp.where(i1 < i2, w_second, w_first)
    return cls, g_lo, g_hi


def _out_router_body(a_ref, w_ref, x_ref, gt_ref, g_ref, sh_ref, sc_ref, wrt_ref, rb_ref,
                     cin_ref, tri_ref, *rest, bs, tt, with_tail):
    if with_tail:
        tail_ref, xo_ref, rows_ref, info_ref, cout_ref, cnt_scr = rest
    else:
        xo_ref, rows_ref, info_ref, cout_ref, cnt_scr = rest
    tm = bs * tt
    first = (pl.program_id(0) == 0) & (pl.program_id(1) == 0)

    @pl.when(first)
    def _():
        cnt_scr[...] = cin_ref[...]

    is_last_t = pl.program_id(1) == pl.num_programs(1) - 1
    split_seqs = bs >= ROUTER_SPLIT
    hb = bs // ROUTER_SPLIT if split_seqs else bs
    ht = tt if split_seqs else tt // ROUTER_SPLIT
    hm = hb * ht
    wrt = wrt_ref[...]
    crow = lax.broadcasted_iota(i32, (CLASS_ROWS, hm), 0)
    irow = lax.broadcasted_iota(i32, (8, hm), 0)
    grow = lax.broadcasted_iota(i32, (ROUTER_ROWS, hm), 0)
    base = cnt_scr[:, 0:1]
    for part in range(ROUTER_SPLIT):
        bsl = slice(part * hb, (part + 1) * hb) if split_seqs else slice(None)
        tsl = slice(None) if split_seqs else slice(part * ht, (part + 1) * ht)
        rsl = slice(part * hm, (part + 1) * hm)
        y = _dot(a_ref[bsl, tsl, :].reshape(hm, D_MODEL), w_ref[...])
        xn = _gated_add(x_ref[bsl, tsl, :], gt_ref[bsl], y.reshape(hb, ht, D_MODEL))
        if with_tail and (split_seqs or part == ROUTER_SPLIT - 1):
            patched = jnp.concatenate([xn[:, :ht - TAIL_ROWS], tail_ref[bsl]], axis=1)
            xn = jnp.where(is_last_t, patched, xn)
        xo_ref[bsl, tsl, :] = xn
        h = _norm_mod(xn, g_ref[...], sh_ref[bsl], sc_ref[bsl]).reshape(hm, D_MODEL)
        h_hi, h_lo = _split2(h)
        p_hi = _dot_nt(wrt, h_hi)
        p_lo = _dot_nt(wrt[:ROUTER_ROWS], h_lo)
        logit = p_hi[:ROUTER_ROWS] + p_hi[ROUTER_ROWS:] + p_lo + rb_ref[...]
        cls, g_lo, g_hi = _route_rows(logit)

        onehot = (crow == cls).astype(f32)
        before = _dot(onehot.astype(bf16), tri_ref[...])
        rank = jnp.sum(onehot * (before + base), axis=0, keepdims=True).astype(i32)
        base = base + jnp.sum(onehot, axis=1, keepdims=True)
        info_ref[0, :, rsl] = jnp.where(irow == 0, cls, jnp.where(irow == 1, rank, 0))

        gates_t = jnp.where(grow == 0, g_lo, jnp.where(grow == 1, g_hi, 0.0))
        rows_ref[rsl, :HALF] = _pack_rows(h)
        rows_ref[rsl, HALF:] = pltpu.bitcast(gates_t.T, u32)
    cnt_new = jnp.broadcast_to(base, cnt_scr.shape)
    cnt_scr[...] = cnt_new
    cout_ref[...] = cnt_new


def _out_router(a, w_bf, x, gt, gain, sh, sc, wrt, rbias, cnt_in, tri, *, bs, tt, tail=None):
    bsz, seq, d = x.shape
    tm = bs * tt
    nt = seq // tt
    ntiles = (bsz // bs) * nt
    with_tail = tail is not None
    body = functools.partial(_out_router_body, bs=bs, tt=tt, with_tail=with_tail)
    tail_specs = [pl.BlockSpec((bs, TAIL_ROWS, d), lambda b, t: (b, 0, 0))] if with_tail else []
    tail_args = [tail] if with_tail else []
    return pl.pallas_call(
        body,
        grid=(bsz // bs, nt),
        in_specs=[
            pl.BlockSpec((bs, tt, d), lambda b, t: (b, t, 0)),
            pl.BlockSpec((d, d), lambda b, t: (0, 0), pipeline_mode=pl.Buffered(1)),
            pl.BlockSpec((bs, tt, d), lambda b, t: (b, t, 0)),
            pl.BlockSpec((bs, SUBLANES, d), lambda b, t: (b, 0, 0)),
            pl.BlockSpec((1, d), lambda b, t: (0, 0)),
            pl.BlockSpec((bs, SUBLANES, d), lambda b, t: (b, 0, 0)),
            pl.BlockSpec((bs, SUBLANES, d), lambda b, t: (b, 0, 0)),
            pl.BlockSpec((2 * ROUTER_ROWS, d), lambda b, t: (0, 0)),
            pl.BlockSpec((ROUTER_ROWS, 1), lambda b, t: (0, 0)),
            pl.BlockSpec((CLASS_ROWS, 128), lambda b, t: (0, 0)),
            pl.BlockSpec((tm // ROUTER_SPLIT, tm // ROUTER_SPLIT), lambda b, t: (0, 0)),
        ] + tail_specs,
        out_specs=[
            pl.BlockSpec((bs, tt, d), lambda b, t: (b, t, 0)),
            pl.BlockSpec((tm, ROW_WORDS), lambda b, t: (b * nt + t, 0)),
            pl.BlockSpec((1, 8, tm), lambda b, t: (b * nt + t, 0, 0)),
            pl.BlockSpec((CLASS_ROWS, 128), lambda b, t: (0, 0)),
        ],
        out_shape=[
            jax.ShapeDtypeStruct((bsz, seq, d), f32),
            jax.ShapeDtypeStruct((bsz * seq, ROW_WORDS), u32),
            jax.ShapeDtypeStruct((ntiles, 8, tm), i32),
            jax.ShapeDtypeStruct((CLASS_ROWS, 128), f32),
        ],
        scratch_shapes=[pltpu.VMEM((CLASS_ROWS, 128), f32)],
        compiler_params=_cparams(("arbitrary", "arbitrary")),
        name="out_router",
    )(a, w_bf, x, gt, gain, sh, sc, wrt, rbias, cnt_in, tri, *tail_args)


def _row_copy_scatter(src_ref, dst_ref, pos_ref, sem, r):
    return pltpu.make_async_copy(src_ref.at[pl.ds(r, 1), :], dst_ref.at[pl.ds(pos_ref[0, 0, r], 1), :], sem)


def _dispatch_body(pos_ref, rows_ref, dst_in_ref, dst_ref, sem, *, tm):
    del dst_in_ref

    def start(r, carry):
        _row_copy_scatter(rows_ref, dst_ref, pos_ref, sem, r).start()
        return carry

    lax.fori_loop(0, tm, start, 0, unroll=DMA_UNROLL)
    def wait(r, carry):
        _row_copy_scatter(rows_ref, dst_ref, pos_ref, sem, r).wait()
        return carry

    lax.fori_loop(0, tm, wait, 0, unroll=DMA_UNROLL)


def _dispatch(rows, pos3, sorted_rows, *, tm):
    n = rows.shape[0]
    body = functools.partial(_dispatch_body, tm=tm)
    return pl.pallas_call(
        body,
        grid=(n // tm,),
        in_specs=[
            pl.BlockSpec((1, 1, tm), lambda i: (i, 0, 0), memory_space=pltpu.SMEM),
            pl.BlockSpec((tm, ROW_WORDS), lambda i: (i, 0)),
            pl.BlockSpec(memory_space=pl.ANY),
        ],
        out_specs=pl.BlockSpec(memory_space=pl.ANY),
        out_shape=jax.ShapeDtypeStruct(sorted_rows.shape, sorted_rows.dtype),
        scratch_shapes=[pltpu.SemaphoreType.DMA(())],
        input_output_aliases={2: 0},
        compiler_params=_cparams(("arbitrary",)),
        name="moe_dispatch",
    )(pos3, rows, sorted_rows)


def _moe_body(ea_ref, eb_ref, valid_ref, xs_ref, w1a_ref, w3a_ref, w2a_ref, w1b_ref, w3b_ref, w2b_ref,
              ys_ref):
    del ea_ref, eb_ref
    i = pl.program_id(0)

    @pl.when(valid_ref[i] > 0)
    def _():
        left, right = _unpack_rows(xs_ref[:, :HALF])
        x = jnp.concatenate([left.astype(bf16), right.astype(bf16)], axis=1)
        gates = pltpu.bitcast(xs_ref[:, HALF:], f32)

        def expert(w1_ref, w3_ref, w2_ref, gate):
            h1 = _dot(x, w1_ref[0])
            h3 = _dot(x, w3_ref[0])
            hid = (h1 * _sigmoid(h1)) * h3 * gate
            return _dot(hid.astype(bf16), w2_ref[0])

        y = expert(w1a_ref, w3a_ref, w2a_ref, gates[:, 0:1]) + expert(w1b_ref, w3b_ref, w2b_ref, gates[:, 1:2])
        ys_ref[...] = _pack_rows(y)

    @pl.when(valid_ref[i] == 0)
    def _():
        ys_ref[...] = jnp.zeros(ys_ref.shape, u32)


def _moe(sorted_rows, tile_ea, tile_eb, tile_valid, w1_bf, w3_bf, w2_bf, *, tm):
    p = sorted_rows.shape[0]
    d, f = D_MODEL, D_EXPERT

    def wa(i, ea, eb, valid):
        return (ea[i], 0, 0)

    def wb(i, ea, eb, valid):
        return (eb[i], 0, 0)

    grid_spec = pltpu.PrefetchScalarGridSpec(
        num_scalar_prefetch=3,
        grid=(p // tm,),
        in_specs=[
            pl.BlockSpec((tm, ROW_WORDS), lambda i, ea, eb, valid: (i, 0)),
            pl.BlockSpec((1, d, f), wa), pl.BlockSpec((1, d, f), wa), pl.BlockSpec((1, f, d), wa),
            pl.BlockSpec((1, d, f), wb), pl.BlockSpec((1, d, f), wb), pl.BlockSpec((1, f, d), wb),
        ],
        out_specs=pl.BlockSpec((tm, HALF), lambda i, ea, eb, valid: (i, 0)),
    )
    return pl.pallas_call(
        _moe_body,
        grid_spec=grid_spec,
        out_shape=jax.ShapeDtypeStruct((p, HALF), u32),
        compiler_params=_cparams(("arbitrary",)),
        name="moe_experts",
    )(tile_ea, tile_eb, tile_valid, sorted_rows, w1_bf, w3_bf, w2_bf, w1_bf, w3_bf, w2_bf)


def _row_copy_gather(src_ref, dst_ref, pos_ref, sem, r):
    return pltpu.make_async_copy(src_ref.at[pl.ds(pos_ref[0, 0, r], 1), :], dst_ref.at[pl.ds(r, 1), :], sem)


def _combine_body(pos_ref, pos_next_ref, ys_ref, x_ref, gt_ref, fg_ref, o_ref, buf, sems, *, bs, tt, final_norm):
    tm = bs * tt
    nt = pl.num_programs(1)
    step = pl.program_id(0) * nt + pl.program_id(1)
    nsteps = pl.num_programs(0) * nt
    slot = step % 2

    def issue(p_ref, s):
        def start(r, carry):
            _row_copy_gather(ys_ref, buf.at[s], p_ref, sems.at[s], r).start()
            return carry
        lax.fori_loop(0, tm, start, 0, unroll=DMA_UNROLL)

    @pl.when(step == 0)
    def _():
        issue(pos_ref, 0)

    @pl.when(step + 1 < nsteps)
    def _():
        issue(pos_next_ref, 1 - slot)

    def wait(r, carry):
        _row_copy_gather(ys_ref, buf.at[slot], pos_ref, sems.at[slot], r).wait()
        return carry

    lax.fori_loop(0, tm, wait, 0, unroll=DMA_UNROLL)

    left, right = _unpack_rows(buf[slot])
    y = jnp.concatenate([left, right], axis=1).reshape(bs, tt, D_MODEL)
    xn = _gated_add(x_ref[...], gt_ref[...], y)
    if final_norm:
        x2 = xn.reshape(tm, D_MODEL)
        xn = (x2 * _inv_rms(x2) * fg_ref[...]).reshape(bs, tt, D_MODEL)
    o_ref[...] = xn


def _combine(ys, pos3, x, gt, final_g, *, bs, tt, final_norm):
    bsz, seq, d = x.shape
    tm = bs * tt
    nt = seq // tt
    nsteps = (bsz // bs) * nt
    body = functools.partial(_combine_body, bs=bs, tt=tt, final_norm=final_norm)
    return pl.pallas_call(
        body,
        grid=(bsz // bs, nt),
        in_specs=[
            pl.BlockSpec((1, 1, tm), lambda b, t: (b * nt + t, 0, 0), memory_space=pltpu.SMEM),
            pl.BlockSpec((1, 1, tm), lambda b, t: (jnp.minimum(b * nt + t + 1, nsteps - 1), 0, 0),
                         memory_space=pltpu.SMEM),
            pl.BlockSpec(memory_space=pl.ANY),
            pl.BlockSpec((bs, tt, d), lambda b, t: (b, t, 0)),
            pl.BlockSpec((bs, SUBLANES, d), lambda b, t: (b, 0, 0)),
            pl.BlockSpec((1, d), lambda b, t: (0, 0)),
        ],
        out_specs=pl.BlockSpec((bs, tt, d), lambda b, t: (b, t, 0)),
        out_shape=jax.ShapeDtypeStruct((bsz, seq, d), f32),
        scratch_shapes=[pltpu.VMEM((2, tm, HALF), u32), pltpu.SemaphoreType.DMA((2,))],
        compiler_params=_cparams(("arbitrary", "arbitrary")),
        name="moe_combine",
    )(pos3, pos3, ys, x, gt, final_g)


def _seg_sum(x, ones_bd):
    hi, lo = _split2(x)
    return _dot(hi, ones_bd) + _dot(lo, ones_bd)


def _rwkv_proj_body(x_ref, sh_ref, sc_ref, g_ref, mu_ref, shift_ref,
                    wr_ref, wk_ref, wv_ref, w1_ref, a1_ref, g1_ref, w2_ref, a2_ref, g2_ref,
                    vec_ref, ones_ref,
                    r_o, k_o, v_o, kk_o, b_o, g_o, lw_o, sh_o,
                    hs_scr, mix_scr, l1w_scr, l1a_scr, l1g_scr, *, bs, tt, tn):
    tm = bs * tt
    t = pl.program_id(1)
    j = pl.program_id(2)
    prev_row = CARRY_ROWS - 1

    @pl.when(j == 0)
    def _():
        @pl.when(t == 0)
        def _():
            hs_scr[:, prev_row:CARRY_ROWS, :] = shift_ref[...]

        h = _norm_mod(x_ref[...], g_ref[...], sh_ref[...], sc_ref[...])
        hs_scr[:, CARRY_ROWS:, :] = h
        h_prev = hs_scr[:, prev_row:prev_row + tt, :]
        h4 = h.reshape(bs, tt // SUBLANES, SUBLANES, D_MODEL)
        xx4 = h_prev.reshape(h4.shape) - h4
        for m in range(N_SHIFT_MIX):
            mix_scr[m] = (h4 + xx4 * mu_ref[m]).reshape(tm, D_MODEL).astype(bf16)
        last = hs_scr[:, prev_row + tt:CARRY_ROWS + tt, :]
        hs_scr[:, prev_row:CARRY_ROWS, :] = last
        sh_o[...] = last
        l1w_scr[...] = jnp.tanh(_dot(mix_scr[1], w1_ref[...])).astype(bf16)
        l1a_scr[...] = _dot(mix_scr[4], a1_ref[...]).astype(bf16)
        l1g_scr[...] = _sigmoid(_dot(mix_scr[5], g1_ref[...])).astype(bf16)

    vec = vec_ref[...]
    w0, a0, k_k, k_a = (vec[i * SUBLANES:(i + 1) * SUBLANES] for i in range(4))
    tiles = lambda x: x.reshape(tm // SUBLANES, SUBLANES, tn)
    r = _dot(mix_scr[0], wr_ref[...])
    k = tiles(_dot(mix_scr[2], wk_ref[...]))
    v = _dot(mix_scr[3], wv_ref[...])
    wl = w0 + tiles(_dot(l1w_scr[...], w2_ref[...]))
    a = _sigmoid(a0 + tiles(_dot(l1a_scr[...], a2_ref[...])))
    g = _dot(l1g_scr[...], g2_ref[...])
    neg = -wl
    softplus = jnp.maximum(neg, 0.0) + jnp.log(1.0 + jnp.exp(-jnp.abs(neg)))
    w_log = -softplus - 0.5
    lw = -jnp.exp(w_log)
    kkr = k * k_k
    ss = tiles(_dot((kkr * kkr).reshape(tm, tn).astype(bf16), ones_ref[...]))
    kk = kkr * lax.rsqrt(jnp.maximum(ss, 1e-24))
    k2 = k * (1.0 + (a - 1.0) * k_a)
    shp = (bs, tt, tn)
    r_o[...] = r.reshape(shp).astype(bf16)
    k_o[...] = k2.reshape(shp).astype(bf16)
    v_o[...] = v.reshape(shp).astype(bf16)
    kk_o[...] = kk.reshape(shp).astype(bf16)
    b_o[...] = (kk * a).reshape(shp).astype(bf16)
    g_o[...] = g.reshape(shp).astype(bf16)
    lw_o[...] = lw.reshape(shp)


def _rwkv_proj(x, sh, sc, gain, mu, shift, wr, wk, wv, w1p, a1p, g1, w2p, a2p, g2, vec, ones_bd,
               *, bs, tt, tn=256):
    bsz, seq, d = x.shape
    tm = bs * tt
    dg = g1.shape[1]
    body = functools.partial(_rwkv_proj_body, bs=bs, tt=tt, tn=tn)
    const2 = lambda b, t, j: (0, 0)
    colblk = lambda b, t, j: (0, j)
    tok = lambda b, t, j: (b, t, j)
    act = lambda dt: jax.ShapeDtypeStruct((bsz, seq, d), dt)
    return pl.pallas_call(
        body,
        grid=(bsz // bs, seq // tt, d // tn),
        in_specs=[
            pl.BlockSpec((bs, tt, d), lambda b, t, j: (b, t, 0)),
            pl.BlockSpec((bs, SUBLANES, d), lambda b, t, j: (b, 0, 0)),
            pl.BlockSpec((bs, SUBLANES, d), lambda b, t, j: (b, 0, 0)),
            pl.BlockSpec((1, d), const2),
            pl.BlockSpec((N_SHIFT_MIX, SUBLANES, d), lambda b, t, j: (0, 0, 0)),
            pl.BlockSpec((bs, 1, d), lambda b, t, j: (b, 0, 0)),
            pl.BlockSpec((d, tn), colblk), pl.BlockSpec((d, tn), colblk), pl.BlockSpec((d, tn), colblk),
            pl.BlockSpec((d, LORA_PAD), const2), pl.BlockSpec((d, LORA_PAD), const2),
            pl.BlockSpec((d, dg), const2),
            pl.BlockSpec((LORA_PAD, tn), colblk), pl.BlockSpec((LORA_PAD, tn), colblk),
            pl.BlockSpec((dg, tn), colblk),
            pl.BlockSpec((4 * SUBLANES, tn), colblk),
            pl.BlockSpec((tn, tn), const2),
        ],
        out_specs=[pl.BlockSpec((bs, tt, tn), tok)] * 7 + [pl.BlockSpec((bs, 1, d), lambda b, t, j: (b, 0, 0))],
        out_shape=[act(bf16)] * 6 + [act(f32), jax.ShapeDtypeStruct((bsz, 1, d), f32)],
        scratch_shapes=[
            pltpu.VMEM((bs, tt + CARRY_ROWS, d), f32),
            pltpu.VMEM((N_SHIFT_MIX, tm, d), bf16),
            pltpu.VMEM((tm, LORA_PAD), bf16),
            pltpu.VMEM((tm, LORA_PAD), bf16),
            pltpu.VMEM((tm, dg), bf16),
        ],
        compiler_params=_cparams(("arbitrary", "arbitrary", "arbitrary")),
        name="rwkv_proj",
    )(x, sh, sc, gain, mu, shift, wr, wk, wv, w1p, a1p, g1, w2p, a2p, g2, vec, ones_bd)


def _block_diag_rows(x, heads):
    return jnp.concatenate([jnp.where(m, x, 0.0) for m in heads], axis=1).astype(bf16)


def _bmm(a, b):
    return lax.dot_general(a, b, (((2,), (1,)), ((0,), (0,))), preferred_element_type=f32)


def _bmm_nt(a, b):
    return lax.dot_general(a, b, (((2,), (2,)), ((0,), (0,))), preferred_element_type=f32)


def _wkv_chunk(r, k, v, kk, beta, lw, c, c_end, state, masks):
    heads, strict, incl, eye, bd_mask = masks
    n = WKV_CHUNK
    bd = lambda x: _block_diag_rows(x, heads)
    g_inv = jnp.exp(-c)
    g_end = jnp.exp(c_end - c)
    a_t = -(kk * jnp.exp(c - lw))
    r_t = r * jnp.exp(c)
    ar = jnp.concatenate([a_t, r_t], axis=1).astype(bf16)
    a12 = _bmm_nt(ar, jnp.concatenate([bd(beta * g_inv), bd(k * g_inv)], axis=1))
    a_ab = jnp.where(strict, a12[:, :n, :PAIR], 0.0)
    a_rb = jnp.where(incl, a12[:, n:, :PAIR], 0.0)
    a_ak = jnp.where(strict, a12[:, :n, PAIR:], 0.0)
    a_rk = jnp.where(incl, a12[:, n:, PAIR:], 0.0)
    from_v = _bmm(jnp.concatenate([a_ak, a_rk], axis=1).astype(bf16), bd(v))
    av, ov = from_v[:, :n], from_v[:, n:]
    q = a_ab
    p = eye + q
    q = _bmm(q.astype(bf16), bd(q))
    for level in range(5):
        if level < 4:
            res = _bmm(jnp.concatenate([p, q], axis=1).astype(bf16), bd(q))
            p = p + res[:, :n]
            q = res[:, n:]
        else:
            p = p + _bmm(p.astype(bf16), bd(q))
    p_bf = p.astype(bf16)
    both = _bmm(p_bf, jnp.concatenate([bd(a_t), bd(av)], axis=2))
    a_bar, u0 = both[:, :, :PAIR], both[:, :, PAIR:]
    s_bf = state.astype(bf16)
    from_state = _bmm_nt(jnp.concatenate([a_bar, r_t], axis=1).astype(bf16), s_bf)
    u = from_state[:, :n] + u0
    o = from_state[:, n:] + ov + _bmm(a_rb.astype(bf16), bd(u))
    uv_t = jnp.swapaxes(jnp.concatenate([u, v], axis=1), 1, 2).astype(bf16)
    bk = jnp.concatenate([beta * g_end, k * g_end], axis=1).astype(bf16)
    decay = jnp.exp(c_end)
    new_state = (state * jnp.concatenate([decay] * WKV_HEADS, axis=1)
                 + jnp.where(bd_mask, _bmm(uv_t, bk), 0.0))
    return o, new_state


def _wkv_body(r_ref, k_ref, v_ref, kk_ref, b_ref, g_ref, lw_ref, s0_ref, vec_ref, tril_ref, ones_ref,
              z_ref, so_ref, s_scr, *, tc):
    t = pl.program_id(1)
    n = WKV_CHUNK

    @pl.when(t == 0)
    def _():
        s_scr[...] = s0_ref[0]

    lane = lax.broadcasted_iota(i32, (n, PAIR), 1)
    row = lax.broadcasted_iota(i32, (n, PAIR), 0)
    head_of = lambda idx: jnp.right_shift(idx, HEAD_SIZE.bit_length() - 1)
    heads = [head_of(lane) == h for h in range(WKV_HEADS)]
    within = lane & (HEAD_SIZE - 1)
    strict = within < row
    incl = within <= row
    eye = (within == row).astype(f32)
    bd_mask = (head_of(lax.broadcasted_iota(i32, (PAIR, PAIR), 0))
               == head_of(lax.broadcasted_iota(i32, (PAIR, PAIR), 1)))
    masks = (heads, strict, incl, eye, bd_mask)
    tril = tril_ref[...]
    ones_bd = ones_ref[...]
    rows = min(tc, n)
    nchunks = max(tc // n, 1)

    def units(x):
        return jnp.stack([x[:, p * PAIR:(p + 1) * PAIR] for p in range(N_PAIRS)], axis=0)

    def seg_sum(x):
        wide = jnp.concatenate([jnp.concatenate([x[2 * i], x[2 * i + 1]], axis=1)
                                for i in range(N_PAIRS // 2)], axis=0)
        s = _dot(wide.astype(bf16), ones_bd)
        return jnp.stack([s[(p // 2) * n:(p // 2 + 1) * n, (p % 2) * PAIR:(p % 2 + 1) * PAIR]
                          for p in range(N_PAIRS)], axis=0)

    vec = units(vec_ref[...])
    ln_w, ln_b, r_k = (vec[:, i * SUBLANES:(i + 1) * SUBLANES] for i in range(3))

    def load(ref, ci):
        x = ref[0, ci * n:ci * n + rows, :].astype(f32)
        if rows < n:
            x = jnp.concatenate([x, jnp.zeros((n - rows, x.shape[1]), f32)], axis=0)
        return x

    for ci in range(nchunks):
        lw = load(lw_ref, ci)
        hi, lo = _split2(lw)
        sums = _dot(tril, hi) + _dot(tril, lo)
        total = jnp.broadcast_to(sums[n - 1:n, :], sums.shape)
        r, k, v = (units(load(ref, ci)) for ref in (r_ref, k_ref, v_ref))
        o, s_new = _wkv_chunk(r, k, v, units(load(kk_ref, ci)), units(load(b_ref, ci)), units(lw),
                              units(sums), units(total), s_scr[...], masks)
        s_scr[...] = s_new
        mean = seg_sum(o) * (1.0 / HEAD_SIZE)
        dev = o - mean
        var = seg_sum(dev * dev) * (1.0 / HEAD_SIZE)
        y = _per_seq(dev * lax.rsqrt(var + GN_EPS), lambda y4, w4, b4: y4 * w4 + b4, ln_w, ln_b)
        y = y + seg_sum(_per_seq(r * k, lambda x4, rk4: x4 * rk4, r_k)) * v
        z = (y * units(load(g_ref, ci))).astype(bf16)
        z_ref[0, ci * n:ci * n + rows, :] = jnp.concatenate([z[p, :rows] for p in range(N_PAIRS)], axis=1)

    @pl.when(t == pl.num_programs(1) - 1)
    def _():
        so_ref[0] = s_scr[...]


def _wkv(r, k, v, kk, beta, g, lw, state_bd, vec, tril, ones_bd, *, tc):
    bsz, seq, d = r.shape
    body = functools.partial(_wkv_body, tc=tc)
    tok = pl.BlockSpec((1, tc, d), lambda b, t: (b, t, 0))
    st = pl.BlockSpec((1, N_PAIRS, PAIR, PAIR), lambda b, t: (b, 0, 0, 0))
    return pl.pallas_call(
        body,
        grid=(bsz, seq // tc),
        in_specs=[tok] * 7 + [
            st,
            pl.BlockSpec((3 * SUBLANES, d), lambda b, t: (0, 0)),
            pl.BlockSpec((WKV_CHUNK, WKV_CHUNK), lambda b, t: (0, 0)),
            pl.BlockSpec((2 * PAIR, 2 * PAIR), lambda b, t: (0, 0)),
        ],
        out_specs=[tok, st],
        out_shape=[jax.ShapeDtypeStruct((bsz, seq, d), bf16),
                   jax.ShapeDtypeStruct(state_bd.shape, f32)],
        scratch_shapes=[pltpu.VMEM((N_PAIRS, PAIR, PAIR), f32)],
        compiler_params=_cparams(("arbitrary", "arbitrary")),
        name="wkv",
    )(r, k, v, kk, beta, g, lw, state_bd, vec, tril, ones_bd)


def _block_ones(n):
    idx = jnp.arange(n) // HEAD_SIZE
    return (idx[:, None] == idx[None, :]).astype(bf16)


def _state_to_pairs(s):
    b = s.shape[0]
    s5 = s.reshape(b, N_PAIRS, WKV_HEADS, HEAD_SIZE, HEAD_SIZE)
    zero = jnp.zeros((b, N_PAIRS, HEAD_SIZE, HEAD_SIZE), s.dtype)
    rows = [jnp.concatenate([s5[:, :, h] if j == h else zero for j in range(WKV_HEADS)], axis=-1)
            for h in range(WKV_HEADS)]
    return jnp.concatenate(rows, axis=-2)


def _pairs_to_state(bd):
    b = bd.shape[0]
    blocks = [bd[:, :, h * HEAD_SIZE:(h + 1) * HEAD_SIZE, h * HEAD_SIZE:(h + 1) * HEAD_SIZE]
              for h in range(WKV_HEADS)]
    return jnp.stack(blocks, axis=2).reshape(b, N_HEADS, HEAD_SIZE, HEAD_SIZE)


_PAIR_TABLE = ((0, 1), (0, 2), (0, 3), (1, 2), (1, 3), (2, 3))


def _moe_plan(counts, n_tiles, tm):
    cnt = counts[:N_CLASSES].astype(i32)
    tiles_per = (cnt + tm - 1) // tm
    tile_end = jnp.cumsum(tiles_per)
    tile_start = tile_end - tiles_per
    offsets = tile_start * tm
    tidx = jnp.arange(n_tiles, dtype=i32)
    cls_of_tile = jnp.sum((tidx[:, None] >= tile_end[None, :]).astype(i32), axis=1)
    valid = (cls_of_tile < N_CLASSES).astype(i32)
    last_cls = jnp.max(jnp.where(tiles_per > 0, jnp.arange(N_CLASSES, dtype=i32), 0))
    cls_c = jnp.where(valid > 0, cls_of_tile, last_cls)
    grp = cls_c // 6
    pr = cls_c % 6
    lo_tab = jnp.array([p[0] for p in _PAIR_TABLE], i32)
    hi_tab = jnp.array([p[1] for p in _PAIR_TABLE], i32)
    ea = grp * EXPERTS_PER_GROUP + lo_tab[pr]
    eb = grp * EXPERTS_PER_GROUP + hi_tab[pr]
    return offsets, ea, eb, valid


def _positions(info, offsets):
    cls = info[:, 0, :].reshape(-1)
    rank = info[:, 1, :].reshape(-1)
    return offsets[cls] + rank


def _router_weights(wg, bg, we, be):
    w = jnp.concatenate([wg, we], axis=1).T
    w = jnp.pad(w, ((0, ROUTER_ROWS - w.shape[0]), (0, 0)))
    hi, lo = _split2_outside(w)
    bias = jnp.pad(jnp.concatenate([bg, be]), (0, ROUTER_ROWS - N_GROUPS - N_EXPERTS))
    return jnp.concatenate([hi, lo], axis=0), bias.reshape(ROUTER_ROWS, 1)


def _tiles(bsz, seq, rows):
    tt = min(seq, rows)
    bs = max(rows // tt, 1)
    assert bsz % bs == 0 and seq % tt == 0
    return bs, tt


def kernel(x_prompt, x_sample, c_prompt, c_sample, state_conv, state_shift, state_wkv, ada_w, ada_b, norm_g, final_g, cv_in, cv_w, cv_out, rw_mu, rw_r, rw_k, rw_v, rw_o, rw_w0, rw_w1, rw_w2, rw_a0, rw_a1, rw_a2, rw_g1, rw_g2, rw_kk, rw_ka, rw_rk, rw_lnw, rw_lnb, moe_wg, moe_bg, moe_we, moe_be, moe_w1, moe_w3, moe_w2):
    d = D_MODEL
    xs = [x_prompt, x_sample]
    bszs = [x.shape[0] for x in xs]
    seqs = [x.shape[1] for x in xs]
    ntok = [b * s for b, s in zip(bszs, seqs)]
    n_total = sum(ntok)

    c_all = jnp.concatenate([c_prompt, c_sample], axis=0)
    rows_c = c_all.shape[0]
    rows_pad = -(-rows_c // 8) * 8
    mod = _ada(jnp.pad(c_all, ((0, rows_pad - rows_c), (0, 0))), ada_w, ada_b)

    def mods(layer, trunk):
        lo = 0 if trunk == 0 else bszs[0]
        m = mod[layer, lo:lo + bszs[trunk]]
        return [jnp.broadcast_to(m[:, None, i * d:(i + 1) * d], (bszs[trunk], SUBLANES, d))
                for i in range(N_ADA)]

    conv_states = [jnp.zeros((1, bszs[0], CONV_WIDTH - 1, d), f32), state_conv]
    shift_states = [jnp.zeros((1, bszs[0], d), f32), state_shift]
    wkv_states = [jnp.zeros((1, bszs[0], N_HEADS, HEAD_SIZE, HEAD_SIZE), f32), state_wkv]

    tri_cache = {}

    def tri(n):
        if n not in tri_cache:
            tri_cache[n] = (jnp.arange(n)[:, None] < jnp.arange(n)[None, :]).astype(bf16)
        return tri_cache[n]

    n_tiles = n_total // MOE_TILE + N_CLASSES
    p_rows = n_tiles * MOE_TILE
    expert_w = [w.reshape((-1,) + w.shape[2:]).astype(bf16) for w in (moe_w1, moe_w3, moe_w2)]

    def moe_layer(layer, rows_list, info_list, counts):
        offsets, ea, eb, valid = _moe_plan(counts[:, 0], n_tiles, MOE_TILE)
        pos_list = [_positions(info, offsets) for info in info_list]
        sorted_rows = jnp.zeros((p_rows, ROW_WORDS), u32)
        for rows, pos in zip(rows_list, pos_list):
            sorted_rows = _dispatch(rows, pos.reshape(-1, 1, MOE_TILE), sorted_rows, tm=MOE_TILE)
        first = layer * N_EXPERTS
        ys = _moe(sorted_rows, ea + first, eb + first, valid, *expert_w, tm=MOE_TILE)
        return ys, pos_list

    def out_router_pair(layer, acts, w_bf, xcur, tails=(None, None)):
        wrt, rbias = _router_weights(moe_wg[layer], moe_bg[layer], moe_we[layer], moe_be[layer])
        counts = jnp.zeros((CLASS_ROWS, 128), f32)
        x_new, rows_list, info_list = [], [], []
        for trunk in range(2):
            _, _, gt_m, sh_f, sc_f, _ = mods(layer, trunk)
            bs, tt = _tiles(bszs[trunk], seqs[trunk], ROUTER_TILE)
            xo, rows, info, counts = _out_router(
                acts[trunk], w_bf, xcur[trunk], gt_m, norm_g[layer, 1].reshape(1, d), sh_f, sc_f,
                wrt, rbias, counts, tri(bs * tt // ROUTER_SPLIT), bs=bs, tt=tt, tail=tails[trunk])
            x_new.append(xo)
            rows_list.append(rows)
            info_list.append(info)
        return x_new, rows_list, info_list, counts

    def combine_pair(layer, ys, pos_list, xcur, final):
        out = []
        for trunk in range(2):
            gt_f = mods(layer, trunk)[5]
            bs, tt = _tiles(bszs[trunk], seqs[trunk], COMBINE_TILE)
            out.append(_combine(ys, pos_list[trunk].reshape(-1, 1, bs * tt), xcur[trunk], gt_f,
                                final_g.reshape(1, d), bs=bs, tt=tt, final_norm=final))
        return out

    w_in_hi, w_in_lo = _split2_outside(cv_in[0])
    w_out_hi, w_out_lo = _split2_outside(cv_out[0])
    gain0 = norm_g[0, 0].reshape(1, d)
    conv_taps = jnp.repeat(cv_w[0], SUBLANES, axis=0)
    z0, conv_out = [], []
    for trunk in range(2):
        sh_m, sc_m = mods(0, trunk)[:2]
        bs, tt = _tiles(bszs[trunk], seqs[trunk], PROJ_TILE)
        st8 = jnp.pad(conv_states[trunk][0], ((0, 0), (CARRY_ROWS - (CONV_WIDTH - 1), 0), (0, 0)))
        z, so = _conv_in(xs[trunk], sh_m, sc_m, gain0, cv_in[0].astype(bf16), conv_taps, st8, bs=bs, tt=tt)
        z0.append(z)
        conv_out.append(so[:, -1, CARRY_ROWS - (CONV_WIDTH - 1):, :][None])
    assert min(seqs) >= TAIL_WINDOW
    x_tail = jnp.concatenate([x[:, -TAIL_WINDOW:, :] for x in xs], axis=0)
    n_seq = x_tail.shape[0]
    mod_tail = [jnp.concatenate([mods(0, trunk)[i] for trunk in range(2)], axis=0) for i in range(3)]
    z_tail, _ = _conv_in(x_tail, mod_tail[0], mod_tail[1], gain0, w_in_hi, conv_taps,
                         jnp.zeros((n_seq, CARRY_ROWS, d), f32), bs=n_seq, tt=TAIL_WINDOW, w_in_lo=w_in_lo)
    x1_tail = _tail_out(z_tail, x_tail, mod_tail[2], w_out_hi, w_out_lo)[:, TAIL_WINDOW - TAIL_ROWS:, :]
    tails = (x1_tail[:bszs[0]], x1_tail[bszs[0]:])
    x1, rows0, info0, counts0 = out_router_pair(0, z0, cv_out[0].astype(bf16), xs, tails)
    ys0, pos0 = moe_layer(0, rows0, info0, counts0)
    x2 = combine_pair(0, ys0, pos0, x1, False)

    pad_l = ((0, 0), (0, LORA_PAD - rw_w1.shape[2]))
    pad_r = ((0, LORA_PAD - rw_w2.shape[1]), (0, 0))
    w1p = jnp.pad(rw_w1[0], pad_l).astype(bf16)
    a1p = jnp.pad(rw_a1[0], pad_l).astype(bf16)
    w2p = jnp.pad(rw_w2[0], pad_r).astype(bf16)
    a2p = jnp.pad(rw_a2[0], pad_r).astype(bf16)
    vec_p = jnp.repeat(jnp.stack([rw_w0[0], rw_a0[0], rw_kk[0], rw_ka[0]]), SUBLANES, axis=0)
    vec_w = jnp.repeat(jnp.stack([rw_lnw[0], rw_lnb[0], rw_rk[0].reshape(d)]), SUBLANES, axis=0)
    wr_bf, wk_bf, wv_bf = rw_r[0].astype(bf16), rw_k[0].astype(bf16), rw_v[0].astype(bf16)
    g1_bf, g2_bf = rw_g1[0].astype(bf16), rw_g2[0].astype(bf16)
    proj_tn = 256
    tril = (jnp.arange(WKV_CHUNK)[:, None] >= jnp.arange(WKV_CHUNK)[None, :]).astype(bf16)
    z1, shift_out, wkv_out = [], [], []
    for trunk in range(2):
        sh_m, sc_m = mods(1, trunk)[:2]
        bs, tt = _tiles(bszs[trunk], seqs[trunk], PROJ_TILE)
        r, k, v, kk, beta, g, lw, sho = _rwkv_proj(
            x2[trunk], sh_m, sc_m, norm_g[1, 0].reshape(1, d),
            jnp.broadcast_to(rw_mu[0][:, None, :], (N_SHIFT_MIX, SUBLANES, d)),
            shift_states[trunk][0].reshape(bszs[trunk], 1, d), wr_bf, wk_bf, wv_bf, w1p, a1p, g1_bf,
            w2p, a2p, g2_bf, vec_p, _block_ones(proj_tn), bs=bs, tt=tt, tn=proj_tn)
        tc = min(seqs[trunk], WKV_CHUNKS_PER_STEP * WKV_CHUNK)
        z, s_bd = _wkv(r, k, v, kk, beta, g, lw, _state_to_pairs(wkv_states[trunk][0]), vec_w, tril,
                       _block_ones(2 * PAIR), tc=tc)
        z1.append(z)
        shift_out.append(sho.reshape(1, bszs[trunk], d))
        wkv_out.append(_pairs_to_state(s_bd)[None])
    x3, rows1, info1, counts1 = out_router_pair(1, z1, rw_o[0].astype(bf16), x2)
    ys1, pos1 = moe_layer(1, rows1, info1, counts1)
    y = combine_pair(1, ys1, pos1, x3, True)

    return (y[0], y[1], conv_out[0], shift_out[0], wkv_out[0], conv_out[1], shift_out[1], wkv_out[1])
```

```python
import functools

import jax
import jax.numpy as jnp
from jax import lax
from jax.experimental import pallas as pl
from jax.experimental.pallas import tpu as pltpu

f32 = jnp.float32
bf16 = jnp.bfloat16
i32 = jnp.int32
u32 = jnp.uint32

D_MODEL = 2048
LANES = 128
SUBLANES = 8
HEAD_SIZE = 64
N_HEADS = D_MODEL // HEAD_SIZE
WKV_HEADS = 2
PAIR = WKV_HEADS * HEAD_SIZE
N_PAIRS = D_MODEL // PAIR
CONV_WIDTH = 3
N_GROUPS = 4
EXPERTS_PER_GROUP = 4
N_EXPERTS = N_GROUPS * EXPERTS_PER_GROUP
D_EXPERT = D_MODEL // 4
N_ADA = 6
N_SHIFT_MIX = 6
RMS_EPS = 1e-6
GN_EPS = 64e-5
LORA_PAD = 128
N_CLASSES = N_GROUPS * 6
CLASS_ROWS = 32
ROUTER_ROWS = 128
HALF = D_MODEL // 2
ROW_WORDS = HALF + 128
WKV_CHUNK = 64
WKV_CHUNKS_PER_STEP = 4
CARRY_ROWS = 8
TAIL_ROWS = 8
TAIL_WINDOW = 2 * TAIL_ROWS

V7X_VMEM_LIMIT = 56 * 1024 * 1024
MOE_TILE = 512
DMA_UNROLL = 8
ROUTER_TILE = 512
ROUTER_SPLIT = 1
COMBINE_TILE = 512
PROJ_TILE = 512
CONV_TILE = 1024


def _cparams(sem, vmem=V7X_VMEM_LIMIT):
    return pltpu.CompilerParams(dimension_semantics=sem, vmem_limit_bytes=vmem)


def _dot(a, b):
    return jnp.dot(a, b, preferred_element_type=f32)


def _dot_nt(a, b):
    return lax.dot_general(a, b, (((1,), (1,)), ((), ())), preferred_element_type=f32)


def _dot_tn(a, b):
    return lax.dot_general(a, b, (((0,), (0,)), ((), ())), preferred_element_type=f32)


def _split2(x):
    hi = pltpu.bitcast(pltpu.bitcast(x, u32) & jnp.uint32(0xFFFF0000), f32)
    return hi.astype(bf16), (x - hi).astype(bf16)


def _split2_outside(x):
    hi = lax.bitcast_convert_type(lax.bitcast_convert_type(x, u32) & jnp.uint32(0xFFFF0000), f32)
    return hi.astype(bf16), (x - hi).astype(bf16)


def _sigmoid(x):
    return 1.0 / (1.0 + jnp.exp(-x))


def _pack_rows(x):
    hi = pltpu.bitcast(x[:, :HALF].astype(bf16).astype(f32), u32)
    lo = pltpu.bitcast(x[:, HALF:].astype(bf16).astype(f32), u32)
    return (hi & jnp.uint32(0xFFFF0000)) | (lo >> 16)


def _unpack_rows(p):
    left = pltpu.bitcast(p & jnp.uint32(0xFFFF0000), f32)
    right = pltpu.bitcast(p << 16, f32)
    return left, right


def _dot3(a_hi, a_lo, w_hi, w_lo):
    return (_dot(a_hi, w_hi) + _dot(a_lo, w_hi)) + (_dot(a_hi, w_lo) + _dot(a_lo, w_lo))


def _ada_body(c_ref, w_ref, b_ref, o_ref):
    c = c_ref[...]
    s_hi, s_lo = _split2(c * _sigmoid(c))
    w_hi, w_lo = _split2(w_ref[0])
    o_ref[0] = _dot3(s_hi, s_lo, w_hi, w_lo) + b_ref[0]


def _ada(c_all, ada_w, ada_b, tn=1024):
    depth, d, n = ada_w.shape
    rows = c_all.shape[0]
    return pl.pallas_call(
        _ada_body,
        grid=(depth, n // tn),
        in_specs=[
            pl.BlockSpec((rows, d), lambda l, j: (0, 0)),
            pl.BlockSpec((1, d, tn), lambda l, j: (l, 0, j)),
            pl.BlockSpec((1, 1, tn), lambda l, j: (l, 0, j)),
        ],
        out_specs=pl.BlockSpec((1, rows, tn), lambda l, j: (l, 0, j)),
        out_shape=jax.ShapeDtypeStruct((depth, rows, n), f32),
        compiler_params=_cparams(("arbitrary", "arbitrary")),
        name="ada",
    )(c_all, ada_w, ada_b.reshape(depth, 1, n))


def _inv_rms(x2):
    sq = x2 * x2
    part = sq[:, 0:LANES]
    for i in range(1, D_MODEL // LANES):
        part = part + sq[:, i * LANES:(i + 1) * LANES]
    hi, lo = _split2(part)
    ones = jnp.ones((LANES, LANES), bf16)
    ms = (_dot(hi, ones) + _dot(lo, ones)) * (1.0 / D_MODEL)
    rs = lax.rsqrt(ms + RMS_EPS)
    return jnp.concatenate([rs] * (D_MODEL // LANES), axis=1)


def _per_seq(x, fn, *vecs):
    bs, tt, n = x.shape
    x4 = x.reshape(bs, tt // SUBLANES, SUBLANES, n)
    return fn(x4, *(v[:, None] for v in vecs)).reshape(bs, tt, n)


def _gated_add(x, gate, y):
    return _per_seq(y, lambda y4, g4: g4 * y4, gate) + x


def _norm_mod(x, gain, sh, sc):
    bs, tt, d = x.shape
    x2 = x.reshape(bs * tt, d)
    y = (x2 * _inv_rms(x2) * gain).reshape(bs, tt, d)
    return _per_seq(y, lambda y4, sh4, sc4: y4 * (1.0 + sc4) + sh4, sh, sc)


def _conv_in_body(x_ref, sh_ref, sc_ref, g_ref, cw_ref, st_ref, *rest, bs, tt, tn, precise):
    nw = 6 if precise else 3
    w_refs = rest[:nw]
    z_ref, so_ref, h_scr, carry_scr, uext_scr = rest[nw:]
    t = pl.program_id(1)
    j = pl.program_id(2)

    @pl.when(j == 0)
    def _():
        h = _norm_mod(x_ref[...], g_ref[...], sh_ref[...], sc_ref[...]).reshape(bs * tt, D_MODEL)
        if precise:
            h_scr[0], h_scr[1] = _split2(h)
        else:
            h_scr[0] = h.astype(bf16)

    @pl.when(t == 0)
    def _():
        carry_scr[j] = st_ref[...]

    if precise:
        b_gate, c_gate, xv = (_dot3(h_scr[0], h_scr[1], w_refs[i][...], w_refs[i + 3][...]) for i in range(3))
    else:
        b_gate, c_gate, xv = (_dot(h_scr[0], w_refs[i][...]) for i in range(3))
    u = (c_gate * xv).reshape(bs, tt, tn)
    uext_scr[:, 0:CARRY_ROWS, :] = carry_scr[j]
    uext_scr[:, CARRY_ROWS:, :] = u
    cw = cw_ref[...]
    tap = lambda i: cw[i * SUBLANES:(i + 1) * SUBLANES]
    tiles = lambda x: x.reshape(bs, tt // SUBLANES, SUBLANES, tn)
    conv = (tiles(uext_scr[:, CARRY_ROWS - 2:CARRY_ROWS - 2 + tt, :]) * tap(0)
            + tiles(uext_scr[:, CARRY_ROWS - 1:CARRY_ROWS - 1 + tt, :]) * tap(1)
            + tiles(u) * tap(2))
    z_ref[...] = (b_gate.reshape(bs, tt, tn) * conv.reshape(bs, tt, tn)).astype(z_ref.dtype)
    last = uext_scr[:, tt:tt + CARRY_ROWS, :]
    carry_scr[j] = last
    so_ref[:, 0] = last


def _conv_in(x, sh, sc, gain, w_in_hi, conv_w, state8, *, bs, tt, tn=512, w_in_lo=None):
    bsz, seq, d = x.shape
    nj = d // tn
    precise = w_in_lo is not None
    body = functools.partial(_conv_in_body, bs=bs, tt=tt, tn=tn, precise=precise)
    w_specs = [pl.BlockSpec((d, tn), lambda b, t, j, k=k: (0, j + k * nj)) for k in range(3)]
    weights = [w_in_hi] * 3 + ([w_in_lo] * 3 if precise else [])
    return pl.pallas_call(
        body,
        grid=(bsz // bs, seq // tt, nj),
        in_specs=[
            pl.BlockSpec((bs, tt, d), lambda b, t, j: (b, t, 0)),
            pl.BlockSpec((bs, SUBLANES, d), lambda b, t, j: (b, 0, 0)),
            pl.BlockSpec((bs, SUBLANES, d), lambda b, t, j: (b, 0, 0)),
            pl.BlockSpec((1, d), lambda b, t, j: (0, 0)),
            pl.BlockSpec((CONV_WIDTH * SUBLANES, tn), lambda b, t, j: (0, j)),
            pl.BlockSpec((bs, CARRY_ROWS, tn), lambda b, t, j: (b, 0, j)),
        ] + w_specs * (2 if precise else 1),
        out_specs=[
            pl.BlockSpec((bs, tt, tn), lambda b, t, j: (b, t, j)),
            pl.BlockSpec((bs, 1, CARRY_ROWS, tn), lambda b, t, j: (b, t, 0, j)),
        ],
        out_shape=[
            jax.ShapeDtypeStruct((bsz, seq, d), f32 if precise else bf16),
            jax.ShapeDtypeStruct((bsz, seq // tt, CARRY_ROWS, d), f32),
        ],
        scratch_shapes=[
            pltpu.VMEM((2 if precise else 1, bs * tt, d), bf16),
            pltpu.VMEM((nj, bs, CARRY_ROWS, tn), f32),
            pltpu.VMEM((bs, tt + CARRY_ROWS, tn), f32),
        ],
        compiler_params=_cparams(("arbitrary", "arbitrary", "arbitrary")),
        name="conv_in_precise" if precise else "conv_in",
    )(x, sh, sc, gain, conv_w, state8, *weights)


def _tail_out_body(z_ref, x_ref, gt_ref, wh_ref, wl_ref, o_ref, *, bs, tt):
    z_hi, z_lo = _split2(z_ref[...].reshape(bs * tt, D_MODEL))
    y = _dot3(z_hi, z_lo, wh_ref[...], wl_ref[...])
    tn = y.shape[1]
    o_ref[...] = _gated_add(x_ref[...], gt_ref[...], y.reshape(bs, tt, tn))


def _tail_out(z, x, gt, w_hi, w_lo, *, tn=512):
    bsz, tt, d = x.shape
    body = functools.partial(_tail_out_body, bs=bsz, tt=tt)
    return pl.pallas_call(
        body,
        grid=(d // tn,),
        in_specs=[
            pl.BlockSpec((bsz, tt, d), lambda j: (0, 0, 0)),
            pl.BlockSpec((bsz, tt, tn), lambda j: (0, 0, j)),
            pl.BlockSpec((bsz, SUBLANES, tn), lambda j: (0, 0, j)),
            pl.BlockSpec((d, tn), lambda j: (0, j)),
            pl.BlockSpec((d, tn), lambda j: (0, j)),
        ],
        out_specs=pl.BlockSpec((bsz, tt, tn), lambda j: (0, 0, j)),
        out_shape=jax.ShapeDtypeStruct((bsz, tt, d), f32),
        compiler_params=_cparams(("arbitrary",)),
        name="tail_out",
    )(z, x, gt, w_hi, w_lo)


def _route_rows(logit):
    lg = [logit[g:g + 1, :] for g in range(N_GROUPS)]
    le = [logit[N_GROUPS + e:N_GROUPS + e + 1, :] for e in range(N_EXPERTS)]
    gmax = jnp.maximum(jnp.maximum(lg[0], lg[1]), jnp.maximum(lg[2], lg[3]))
    gidx = jnp.where(lg[0] == gmax, 0, jnp.where(lg[1] == gmax, 1, jnp.where(lg[2] == gmax, 2, 3)))
    denom = (jnp.exp(lg[0] - gmax) + jnp.exp(lg[1] - gmax)
             + jnp.exp(lg[2] - gmax) + jnp.exp(lg[3] - gmax))
    p_grp = 1.0 / denom
    leg = [jnp.where(gidx == 0, le[i],
                     jnp.where(gidx == 1, le[EXPERTS_PER_GROUP + i],
                               jnp.where(gidx == 2, le[2 * EXPERTS_PER_GROUP + i],
                                         le[3 * EXPERTS_PER_GROUP + i])))
           for i in range(EXPERTS_PER_GROUP)]
    v1 = jnp.maximum(jnp.maximum(leg[0], leg[1]), jnp.maximum(leg[2], leg[3]))
    i1 = jnp.where(leg[0] == v1, 0, jnp.where(leg[1] == v1, 1, jnp.where(leg[2] == v1, 2, 3)))
    neg = jnp.float32(-jnp.inf)
    rest = [jnp.where(i1 == i, neg, leg[i]) for i in range(EXPERTS_PER_GROUP)]
    v2 = jnp.maximum(jnp.maximum(rest[0], rest[1]), jnp.maximum(rest[2], rest[3]))
    i2 = jnp.where((rest[0] == v2) & (i1 != 0), 0,
                   jnp.where((rest[1] == v2) & (i1 != 1), 1,
                             jnp.where((rest[2] == v2) & (i1 != 2), 2, 3)))
    s = jnp.exp(v2 - v1)
    w_first = p_grp / (1.0 + s)
    w_second = p_grp * s / (1.0 + s)
    i_lo = jnp.minimum(i1, i2)
    i_hi = jnp.maximum(i1, i2)
    pair_base = jnp.where(i_lo == 0, 0, jnp.where(i_lo == 1, 3, 5))
    cls = gidx * 6 + pair_base + (i_hi - i_lo - 1)
    g_lo = jnp.where(i1 < i2, w_first, w_second)
    g_hi = jnp.where(i1 < i2, w_second, w_first)
    return cls, g_lo, g_hi


def _out_router_body(a_ref, w_ref, x_ref, gt_ref, g_ref, sh_ref, sc_ref, wrt_ref, rb_ref,
                     cin_ref, tri_ref, *rest, bs, tt, with_tail):
    if with_tail:
        tail_ref, xo_ref, rows_ref, info_ref, cout_ref, cnt_scr = rest
    else:
        xo_ref, rows_ref, info_ref, cout_ref, cnt_scr = rest
    tm = bs * tt
    first = (pl.program_id(0) == 0) & (pl.program_id(1) == 0)

    @pl.when(first)
    def _():
        cnt_scr[...] = cin_ref[...]

    is_last_t = pl.program_id(1) == pl.num_programs(1) - 1
    split_seqs = bs >= ROUTER_SPLIT
    hb = bs // ROUTER_SPLIT if split_seqs else bs
    ht = tt if split_seqs else tt // ROUTER_SPLIT
    hm = hb * ht
    wrt = wrt_ref[...]
    crow = lax.broadcasted_iota(i32, (CLASS_ROWS, hm), 0)
    irow = lax.broadcasted_iota(i32, (8, hm), 0)
    grow = lax.broadcasted_iota(i32, (ROUTER_ROWS, hm), 0)
    base = cnt_scr[:, 0:1]
    for part in range(ROUTER_SPLIT):
        bsl = slice(part * hb, (part + 1) * hb) if split_seqs else slice(None)
        tsl = slice(None) if split_seqs else slice(part * ht, (part + 1) * ht)
        rsl = slice(part * hm, (part + 1) * hm)
        y = _dot(a_ref[bsl, tsl, :].reshape(hm, D_MODEL), w_ref[...])
        xn = _gated_add(x_ref[bsl, tsl, :], gt_ref[bsl], y.reshape(hb, ht, D_MODEL))
        if with_tail and (split_seqs or part == ROUTER_SPLIT - 1):
            patched = jnp.concatenate([xn[:, :ht - TAIL_ROWS], tail_ref[bsl]], axis=1)
            xn = jnp.where(is_last_t, patched, xn)
        xo_ref[bsl, tsl, :] = xn
        h = _norm_mod(xn, g_ref[...], sh_ref[bsl], sc_ref[bsl]).reshape(hm, D_MODEL)
        h_hi, h_lo = _split2(h)
        p_hi = _dot_nt(wrt, h_hi)
        p_lo = _dot_nt(wrt, h_lo)
        logit = ((p_hi[:ROUTER_ROWS] + p_hi[ROUTER_ROWS:]) + (p_lo[:ROUTER_ROWS] + p_lo[ROUTER_ROWS:])
                 + rb_ref[...])
        cls, g_lo, g_hi = _route_rows(logit)

        onehot = (crow == cls).astype(f32)
        before = _dot(onehot.astype(bf16), tri_ref[...])
        rank = jnp.sum(onehot * (before + base), axis=0, keepdims=True).astype(i32)
        base = base + jnp.sum(onehot, axis=1, keepdims=True)
        info_ref[0, :, rsl] = jnp.where(irow == 0, cls, jnp.where(irow == 1, rank, 0))

        gates_t = jnp.where(grow == 0, g_lo, jnp.where(grow == 1, g_hi, 0.0))
        rows_ref[rsl, :HALF] = _pack_rows(h)
        rows_ref[rsl, HALF:] = pltpu.bitcast(gates_t.T, u32)
    cnt_new = jnp.broadcast_to(base, cnt_scr.shape)
    cnt_scr[...] = cnt_new
    cout_ref[...] = cnt_new


def _out_router(a, w_bf, x, gt, gain, sh, sc, wrt, rbias, cnt_in, tri, *, bs, tt, tail=None):
    bsz, seq, d = x.shape
    tm = bs * tt
    nt = seq // tt
    ntiles = (bsz // bs) * nt
    with_tail = tail is not None
    body = functools.partial(_out_router_body, bs=bs, tt=tt, with_tail=with_tail)
    tail_specs = [pl.BlockSpec((bs, TAIL_ROWS, d), lambda b, t: (b, 0, 0))] if with_tail else []
    tail_args = [tail] if with_tail else []
    return pl.pallas_call(
        body,
        grid=(bsz // bs, nt),
        in_specs=[
            pl.BlockSpec((bs, tt, d), lambda b, t: (b, t, 0)),
            pl.BlockSpec((d, d), lambda b, t: (0, 0), pipeline_mode=pl.Buffered(1)),
            pl.BlockSpec((bs, tt, d), lambda b, t: (b, t, 0)),
            pl.BlockSpec((bs, SUBLANES, d), lambda b, t: (b, 0, 0)),
            pl.BlockSpec((1, d), lambda b, t: (0, 0)),
            pl.BlockSpec((bs, SUBLANES, d), lambda b, t: (b, 0, 0)),
            pl.BlockSpec((bs, SUBLANES, d), lambda b, t: (b, 0, 0)),
            pl.BlockSpec((2 * ROUTER_ROWS, d), lambda b, t: (0, 0)),
            pl.BlockSpec((ROUTER_ROWS, 1), lambda b, t: (0, 0)),
            pl.BlockSpec((CLASS_ROWS, 128), lambda b, t: (0, 0)),
            pl.BlockSpec((tm // ROUTER_SPLIT, tm // ROUTER_SPLIT), lambda b, t: (0, 0)),
        ] + tail_specs,
        out_specs=[
            pl.BlockSpec((bs, tt, d), lambda b, t: (b, t, 0)),
            pl.BlockSpec((tm, ROW_WORDS), lambda b, t: (b * nt + t, 0)),
            pl.BlockSpec((1, 8, tm), lambda b, t: (b * nt + t, 0, 0)),
            pl.BlockSpec((CLASS_ROWS, 128), lambda b, t: (0, 0)),
        ],
        out_shape=[
            jax.ShapeDtypeStruct((bsz, seq, d), f32),
            jax.ShapeDtypeStruct((bsz * seq, ROW_WORDS), u32),
            jax.ShapeDtypeStruct((ntiles, 8, tm), i32),
            jax.ShapeDtypeStruct((CLASS_ROWS, 128), f32),
        ],
        scratch_shapes=[pltpu.VMEM((CLASS_ROWS, 128), f32)],
        compiler_params=_cparams(("arbitrary", "arbitrary")),
        name="out_router",
    )(a, w_bf, x, gt, gain, sh, sc, wrt, rbias, cnt_in, tri, *tail_args)


def _row_copy_scatter(src_ref, dst_ref, pos_ref, sem, r):
    return pltpu.make_async_copy(src_ref.at[pl.ds(r, 1), :], dst_ref.at[pl.ds(pos_ref[0, 0, r], 1), :], sem)


def _dispatch_body(pos_ref, rows_ref, dst_in_ref, dst_ref, sem, *, tm):
    del dst_in_ref

    def start(r, carry):
        _row_copy_scatter(rows_ref, dst_ref, pos_ref, sem, r).start()
        return carry

    lax.fori_loop(0, tm, start, 0, unroll=DMA_UNROLL)
    def wait(r, carry):
        _row_copy_scatter(rows_ref, dst_ref, pos_ref, sem, r).wait()
        return carry

    lax.fori_loop(0, tm, wait, 0, unroll=DMA_UNROLL)


def _dispatch(rows, pos3, sorted_rows, *, tm):
    n = rows.shape[0]
    body = functools.partial(_dispatch_body, tm=tm)
    return pl.pallas_call(
        body,
        grid=(n // tm,),
        in_specs=[
            pl.BlockSpec((1, 1, tm), lambda i: (i, 0, 0), memory_space=pltpu.SMEM),
            pl.BlockSpec((tm, ROW_WORDS), lambda i: (i, 0)),
            pl.BlockSpec(memory_space=pl.ANY),
        ],
        out_specs=pl.BlockSpec(memory_space=pl.ANY),
        out_shape=jax.ShapeDtypeStruct(sorted_rows.shape, sorted_rows.dtype),
        scratch_shapes=[pltpu.SemaphoreType.DMA(())],
        input_output_aliases={2: 0},
        compiler_params=_cparams(("arbitrary",)),
        name="moe_dispatch",
    )(pos3, rows, sorted_rows)


def _moe_body(ea_ref, eb_ref, valid_ref, xs_ref, w1a_ref, w3a_ref, w2a_ref, w1b_ref, w3b_ref, w2b_ref,
              ys_ref):
    del ea_ref, eb_ref
    i = pl.program_id(0)

    @pl.when(valid_ref[i] > 0)
    def _():
        left, right = _unpack_rows(xs_ref[:, :HALF])
        x = jnp.concatenate([left.astype(bf16), right.astype(bf16)], axis=1)
        gates = pltpu.bitcast(xs_ref[:, HALF:], f32)

        def expert(w1_ref, w3_ref, w2_ref, gate):
            h1 = _dot(x, w1_ref[0])
            h3 = _dot(x, w3_ref[0])
            hid = (h1 * _sigmoid(h1)) * h3 * gate
            return _dot(hid.astype(bf16), w2_ref[0])

        y = expert(w1a_ref, w3a_ref, w2a_ref, gates[:, 0:1]) + expert(w1b_ref, w3b_ref, w2b_ref, gates[:, 1:2])
        ys_ref[...] = _pack_rows(y)

    @pl.when(valid_ref[i] == 0)
    def _():
        ys_ref[...] = jnp.zeros(ys_ref.shape, u32)


def _moe(sorted_rows, tile_ea, tile_eb, tile_valid, w1_bf, w3_bf, w2_bf, *, tm):
    p = sorted_rows.shape[0]
    d, f = D_MODEL, D_EXPERT

    def wa(i, ea, eb, valid):
        return (ea[i], 0, 0)

    def wb(i, ea, eb, valid):
        return (eb[i], 0, 0)

    grid_spec = pltpu.PrefetchScalarGridSpec(
        num_scalar_prefetch=3,
        grid=(p // tm,),
        in_specs=[
            pl.BlockSpec((tm, ROW_WORDS), lambda i, ea, eb, valid: (i, 0)),
            pl.BlockSpec((1, d, f), wa), pl.BlockSpec((1, d, f), wa), pl.BlockSpec((1, f, d), wa),
            pl.BlockSpec((1, d, f), wb), pl.BlockSpec((1, d, f), wb), pl.BlockSpec((1, f, d), wb),
        ],
        out_specs=pl.BlockSpec((tm, HALF), lambda i, ea, eb, valid: (i, 0)),
    )
    return pl.pallas_call(
        _moe_body,
        grid_spec=grid_spec,
        out_shape=jax.ShapeDtypeStruct((p, HALF), u32),
        compiler_params=_cparams(("arbitrary",)),
        name="moe_experts",
    )(tile_ea, tile_eb, tile_valid, sorted_rows, w1_bf, w3_bf, w2_bf, w1_bf, w3_bf, w2_bf)


def _row_copy_gather(src_ref, dst_ref, pos_ref, sem, r):
    return pltpu.make_async_copy(src_ref.at[pl.ds(pos_ref[0, 0, r], 1), :], dst_ref.at[pl.ds(r, 1), :], sem)


def _combine_body(pos_ref, pos_next_ref, ys_ref, x_ref, gt_ref, fg_ref, o_ref, buf, sems, *, bs, tt, final_norm):
    tm = bs * tt
    nt = pl.num_programs(1)
    step = pl.program_id(0) * nt + pl.program_id(1)
    nsteps = pl.num_programs(0) * nt
    slot = step % 2

    def issue(p_ref, s):
        def start(r, carry):
            _row_copy_gather(ys_ref, buf.at[s], p_ref, sems.at[s], r).start()
            return carry
        lax.fori_loop(0, tm, start, 0, unroll=DMA_UNROLL)

    @pl.when(step == 0)
    def _():
        issue(pos_ref, 0)

    @pl.when(step + 1 < nsteps)
    def _():
        issue(pos_next_ref, 1 - slot)

    def wait(r, carry):
        _row_copy_gather(ys_ref, buf.at[slot], pos_ref, sems.at[slot], r).wait()
        return carry

    lax.fori_loop(0, tm, wait, 0, unroll=DMA_UNROLL)

    left, right = _unpack_rows(buf[slot])
    y = jnp.concatenate([left, right], axis=1).reshape(bs, tt, D_MODEL)
    xn = _gated_add(x_ref[...], gt_ref[...], y)
    if final_norm:
        x2 = xn.reshape(tm, D_MODEL)
        xn = (x2 * _inv_rms(x2) * fg_ref[...]).reshape(bs, tt, D_MODEL)
    o_ref[...] = xn


def _combine(ys, pos3, x, gt, final_g, *, bs, tt, final_norm):
    bsz, seq, d = x.shape
    tm = bs * tt
    nt = seq // tt
    nsteps = (bsz // bs) * nt
    body = functools.partial(_combine_body, bs=bs, tt=tt, final_norm=final_norm)
    return pl.pallas_call(
        body,
        grid=(bsz // bs, nt),
        in_specs=[
            pl.BlockSpec((1, 1, tm), lambda b, t: (b * nt + t, 0, 0), memory_space=pltpu.SMEM),
            pl.BlockSpec((1, 1, tm), lambda b, t: (jnp.minimum(b * nt + t + 1, nsteps - 1), 0, 0),
                         memory_space=pltpu.SMEM),
            pl.BlockSpec(memory_space=pl.ANY),
            pl.BlockSpec((bs, tt, d), lambda b, t: (b, t, 0)),
            pl.BlockSpec((bs, SUBLANES, d), lambda b, t: (b, 0, 0)),
            pl.BlockSpec((1, d), lambda b, t: (0, 0)),
        ],
        out_specs=pl.BlockSpec((bs, tt, d), lambda b, t: (b, t, 0)),
        out_shape=jax.ShapeDtypeStruct((bsz, seq, d), f32),
        scratch_shapes=[pltpu.VMEM((2, tm, HALF), u32), pltpu.SemaphoreType.DMA((2,))],
        compiler_params=_cparams(("arbitrary", "arbitrary")),
        name="moe_combine",
    )(pos3, pos3, ys, x, gt, final_g)


def _seg_sum(x, ones_bd):
    hi, lo = _split2(x)
    return _dot(hi, ones_bd) + _dot(lo, ones_bd)


def _rwkv_proj_body(x_ref, sh_ref, sc_ref, g_ref, mu_ref, shift_ref,
                    wr_ref, wk_ref, wv_ref, w1_ref, a1_ref, g1_ref, w2_ref, a2_ref, g2_ref,
                    vec_ref, ones_ref,
                    r_o, k_o, v_o, kk_o, b_o, g_o, lw_o, sh_o,
                    hs_scr, mix_scr, l1w_scr, l1a_scr, l1g_scr, *, bs, tt, tn):
    tm = bs * tt
    t = pl.program_id(1)
    j = pl.program_id(2)
    prev_row = CARRY_ROWS - 1

    @pl.when(j == 0)
    def _():
        @pl.when(t == 0)
        def _():
            hs_scr[:, prev_row:CARRY_ROWS, :] = shift_ref[...]

        h = _norm_mod(x_ref[...], g_ref[...], sh_ref[...], sc_ref[...])
        hs_scr[:, CARRY_ROWS:, :] = h
        h_prev = hs_scr[:, prev_row:prev_row + tt, :]
        h4 = h.reshape(bs, tt // SUBLANES, SUBLANES, D_MODEL)
        xx4 = h_prev.reshape(h4.shape) - h4
        for m in range(N_SHIFT_MIX):
            mix_scr[m] = (h4 + xx4 * mu_ref[m]).reshape(tm, D_MODEL).astype(bf16)
        last = hs_scr[:, prev_row + tt:CARRY_ROWS + tt, :]
        hs_scr[:, prev_row:CARRY_ROWS, :] = last
        sh_o[...] = last
        l1w_scr[...] = jnp.tanh(_dot(mix_scr[1], w1_ref[...])).astype(bf16)
        l1a_scr[...] = _dot(mix_scr[4], a1_ref[...]).astype(bf16)
        l1g_scr[...] = _sigmoid(_dot(mix_scr[5], g1_ref[...])).astype(bf16)

    vec = vec_ref[...]
    w0, a0, k_k, k_a = (vec[i * SUBLANES:(i + 1) * SUBLANES] for i in range(4))
    tiles = lambda x: x.reshape(tm // SUBLANES, SUBLANES, tn)
    r = _dot(mix_scr[0], wr_ref[...])
    k = tiles(_dot(mix_scr[2], wk_ref[...]))
    v = _dot(mix_scr[3], wv_ref[...])
    wl = w0 + tiles(_dot(l1w_scr[...], w2_ref[...]))
    a = _sigmoid(a0 + tiles(_dot(l1a_scr[...], a2_ref[...])))
    g = _dot(l1g_scr[...], g2_ref[...])
    neg = -wl
    softplus = jnp.maximum(neg, 0.0) + jnp.log(1.0 + jnp.exp(-jnp.abs(neg)))
    w_log = -softplus - 0.5
    lw = -jnp.exp(w_log)
    kkr = k * k_k
    ss = tiles(_dot((kkr * kkr).reshape(tm, tn).astype(bf16), ones_ref[...]))
    kk = kkr * lax.rsqrt(jnp.maximum(ss, 1e-24))
    k2 = k * (1.0 + (a - 1.0) * k_a)
    shp = (bs, tt, tn)
    r_o[...] = r.reshape(shp).astype(bf16)
    k_o[...] = k2.reshape(shp).astype(bf16)
    v_o[...] = v.reshape(shp).astype(bf16)
    kk_o[...] = kk.reshape(shp).astype(bf16)
    b_o[...] = (kk * a).reshape(shp).astype(bf16)
    g_o[...] = g.reshape(shp).astype(bf16)
    lw_o[...] = lw.reshape(shp)


def _rwkv_proj(x, sh, sc, gain, mu, shift, wr, wk, wv, w1p, a1p, g1, w2p, a2p, g2, vec, ones_bd,
               *, bs, tt, tn=256):
    bsz, seq, d = x.shape
    tm = bs * tt
    dg = g1.shape[1]
    body = functools.partial(_rwkv_proj_body, bs=bs, tt=tt, tn=tn)
    const2 = lambda b, t, j: (0, 0)
    colblk = lambda b, t, j: (0, j)
    tok = lambda b, t, j: (b, t, j)
    act = lambda dt: jax.ShapeDtypeStruct((bsz, seq, d), dt)
    return pl.pallas_call(
        body,
        grid=(bsz // bs, seq // tt, d // tn),
        in_specs=[
            pl.BlockSpec((bs, tt, d), lambda b, t, j: (b, t, 0)),
            pl.BlockSpec((bs, SUBLANES, d), lambda b, t, j: (b, 0, 0)),
            pl.BlockSpec((bs, SUBLANES, d), lambda b, t, j: (b, 0, 0)),
            pl.BlockSpec((1, d), const2),
            pl.BlockSpec((N_SHIFT_MIX, SUBLANES, d), lambda b, t, j: (0, 0, 0)),
            pl.BlockSpec((bs, 1, d), lambda b, t, j: (b, 0, 0)),
            pl.BlockSpec((d, tn), colblk), pl.BlockSpec((d, tn), colblk), pl.BlockSpec((d, tn), colblk),
            pl.BlockSpec((d, LORA_PAD), const2), pl.BlockSpec((d, LORA_PAD), const2),
            pl.BlockSpec((d, dg), const2),
            pl.BlockSpec((LORA_PAD, tn), colblk), pl.BlockSpec((LORA_PAD, tn), colblk),
            pl.BlockSpec((dg, tn), colblk),
            pl.BlockSpec((4 * SUBLANES, tn), colblk),
            pl.BlockSpec((tn, tn), const2),
        ],
        out_specs=[pl.BlockSpec((bs, tt, tn), tok)] * 7 + [pl.BlockSpec((bs, 1, d), lambda b, t, j: (b, 0, 0))],
        out_shape=[act(bf16)] * 6 + [act(f32), jax.ShapeDtypeStruct((bsz, 1, d), f32)],
        scratch_shapes=[
            pltpu.VMEM((bs, tt + CARRY_ROWS, d), f32),
            pltpu.VMEM((N_SHIFT_MIX, tm, d), bf16),
            pltpu.VMEM((tm, LORA_PAD), bf16),
            pltpu.VMEM((tm, LORA_PAD), bf16),
            pltpu.VMEM((tm, dg), bf16),
        ],
        compiler_params=_cparams(("arbitrary", "arbitrary", "arbitrary")),
        name="rwkv_proj",
    )(x, sh, sc, gain, mu, shift, wr, wk, wv, w1p, a1p, g1, w2p, a2p, g2, vec, ones_bd)


def _block_diag_rows(x, heads):
    return jnp.concatenate([jnp.where(m, x, 0.0) for m in heads], axis=1).astype(bf16)


def _bmm(a, b):
    return lax.dot_general(a, b, (((2,), (1,)), ((0,), (0,))), preferred_element_type=f32)


def _bmm_nt(a, b):
    return lax.dot_general(a, b, (((2,), (2,)), ((0,), (0,))), preferred_element_type=f32)


def _wkv_chunk(r, k, v, kk, beta, lw, c, c_end, state, masks):
    heads, strict, incl, eye, bd_mask = masks
    n = WKV_CHUNK
    bd = lambda x: _block_diag_rows(x, heads)
    g_inv = jnp.exp(-c)
    g_end = jnp.exp(c_end - c)
    a_t = -(kk * jnp.exp(c - lw))
    r_t = r * jnp.exp(c)
    ar = jnp.concatenate([a_t, r_t], axis=1).astype(bf16)
    a12 = _bmm_nt(ar, jnp.concatenate([bd(beta * g_inv), bd(k * g_inv)], axis=1))
    a_ab = jnp.where(strict, a12[:, :n, :PAIR], 0.0)
    a_rb = jnp.where(incl, a12[:, n:, :PAIR], 0.0)
    a_ak = jnp.where(strict, a12[:, :n, PAIR:], 0.0)
    a_rk = jnp.where(incl, a12[:, n:, PAIR:], 0.0)
    from_v = _bmm(jnp.concatenate([a_ak, a_rk], axis=1).astype(bf16), bd(v))
    av, ov = from_v[:, :n], from_v[:, n:]
    q = a_ab
    p = eye + q
    q = _bmm(q.astype(bf16), bd(q))
    for level in range(5):
        if level < 4:
            res = _bmm(jnp.concatenate([p, q], axis=1).astype(bf16), bd(q))
            p = p + res[:, :n]
            q = res[:, n:]
        else:
            p = p + _bmm(p.astype(bf16), bd(q))
    p_bf = p.astype(bf16)
    both = _bmm(p_bf, jnp.concatenate([bd(a_t), bd(av)], axis=2))
    a_bar, u0 = both[:, :, :PAIR], both[:, :, PAIR:]
    s_bf = state.astype(bf16)
    from_state = _bmm_nt(jnp.concatenate([a_bar, r_t], axis=1).astype(bf16), s_bf)
    u = from_state[:, :n] + u0
    o = from_state[:, n:] + ov + _bmm(a_rb.astype(bf16), bd(u))
    uv_t = jnp.swapaxes(jnp.concatenate([u, v], axis=1), 1, 2).astype(bf16)
    bk = jnp.concatenate([beta * g_end, k * g_end], axis=1).astype(bf16)
    decay = jnp.exp(c_end)
    new_state = (state * jnp.concatenate([decay] * WKV_HEADS, axis=1)
                 + jnp.where(bd_mask, _bmm(uv_t, bk), 0.0))
    return o, new_state


def _wkv_body(r_ref, k_ref, v_ref, kk_ref, b_ref, g_ref, lw_ref, s0_ref, vec_ref, tril_ref, ones_ref,
              z_ref, so_ref, s_scr, *, tc):
    t = pl.program_id(1)
    n = WKV_CHUNK

    @pl.when(t == 0)
    def _():
        s_scr[...] = s0_ref[0]

    lane = lax.broadcasted_iota(i32, (n, PAIR), 1)
    row = lax.broadcasted_iota(i32, (n, PAIR), 0)
    head_of = lambda idx: jnp.right_shift(idx, HEAD_SIZE.bit_length() - 1)
    heads = [head_of(lane) == h for h in range(WKV_HEADS)]
    within = lane & (HEAD_SIZE - 1)
    strict = within < row
    incl = within <= row
    eye = (within == row).astype(f32)
    bd_mask = (head_of(lax.broadcasted_iota(i32, (PAIR, PAIR), 0))
               == head_of(lax.broadcasted_iota(i32, (PAIR, PAIR), 1)))
    masks = (heads, strict, incl, eye, bd_mask)
    tril = tril_ref[...]
    ones_bd = ones_ref[...]
    rows = min(tc, n)
    nchunks = max(tc // n, 1)

    def units(x):
        return jnp.stack([x[:, p * PAIR:(p + 1) * PAIR] for p in range(N_PAIRS)], axis=0)

    def seg_sum(x):
        wide = jnp.concatenate([jnp.concatenate([x[2 * i], x[2 * i + 1]], axis=1)
                                for i in range(N_PAIRS // 2)], axis=0)
        s = _dot(wide.astype(bf16), ones_bd)
        return jnp.stack([s[(p // 2) * n:(p // 2 + 1) * n, (p % 2) * PAIR:(p % 2 + 1) * PAIR]
                          for p in range(N_PAIRS)], axis=0)

    vec = units(vec_ref[...])
    ln_w, ln_b, r_k = (vec[:, i * SUBLANES:(i + 1) * SUBLANES] for i in range(3))

    def load(ref, ci):
        x = ref[0, ci * n:ci * n + rows, :].astype(f32)
        if rows < n:
            x = jnp.concatenate([x, jnp.zeros((n - rows, x.shape[1]), f32)], axis=0)
        return x

    for ci in range(nchunks):
        lw = load(lw_ref, ci)
        hi, lo = _split2(lw)
        sums = _dot(tril, hi) + _dot(tril, lo)
        total = jnp.broadcast_to(sums[n - 1:n, :], sums.shape)
        r, k, v = (units(load(ref, ci)) for ref in (r_ref, k_ref, v_ref))
        o, s_new = _wkv_chunk(r, k, v, units(load(kk_ref, ci)), units(load(b_ref, ci)), units(lw),
                              units(sums), units(total), s_scr[...], masks)
        s_scr[...] = s_new
        mean = seg_sum(o) * (1.0 / HEAD_SIZE)
        dev = o - mean
        var = seg_sum(dev * dev) * (1.0 / HEAD_SIZE)
        y = _per_seq(dev * lax.rsqrt(var + GN_EPS), lambda y4, w4, b4: y4 * w4 + b4, ln_w, ln_b)
        y = y + seg_sum(_per_seq(r * k, lambda x4, rk4: x4 * rk4, r_k)) * v
        z = (y * units(load(g_ref, ci))).astype(bf16)
        z_ref[0, ci * n:ci * n + rows, :] = jnp.concatenate([z[p, :rows] for p in range(N_PAIRS)], axis=1)

    @pl.when(t == pl.num_programs(1) - 1)
    def _():
        so_ref[0] = s_scr[...]


def _wkv(r, k, v, kk, beta, g, lw, state_bd, vec, tril, ones_bd, *, tc):
    bsz, seq, d = r.shape
    body = functools.partial(_wkv_body, tc=tc)
    tok = pl.BlockSpec((1, tc, d), lambda b, t: (b, t, 0))
    st = pl.BlockSpec((1, N_PAIRS, PAIR, PAIR), lambda b, t: (b, 0, 0, 0))
    return pl.pallas_call(
        body,
        grid=(bsz, seq // tc),
        in_specs=[tok] * 7 + [
            st,
            pl.BlockSpec((3 * SUBLANES, d), lambda b, t: (0, 0)),
            pl.BlockSpec((WKV_CHUNK, WKV_CHUNK), lambda b, t: (0, 0)),
            pl.BlockSpec((2 * PAIR, 2 * PAIR), lambda b, t: (0, 0)),
        ],
        out_specs=[tok, st],
        out_shape=[jax.ShapeDtypeStruct((bsz, seq, d), bf16),
                   jax.ShapeDtypeStruct(state_bd.shape, f32)],
        scratch_shapes=[pltpu.VMEM((N_PAIRS, PAIR, PAIR), f32)],
        compiler_params=_cparams(("arbitrary", "arbitrary")),
        name="wkv",
    )(r, k, v, kk, beta, g, lw, state_bd, vec, tril, ones_bd)


def _block_ones(n):
    idx = jnp.arange(n) // HEAD_SIZE
    return (idx[:, None] == idx[None, :]).astype(bf16)


def _state_to_pairs(s):
    b = s.shape[0]
    s5 = s.reshape(b, N_PAIRS, WKV_HEADS, HEAD_SIZE, HEAD_SIZE)
    zero = jnp.zeros((b, N_PAIRS, HEAD_SIZE, HEAD_SIZE), s.dtype)
    rows = [jnp.concatenate([s5[:, :, h] if j == h else zero for j in range(WKV_HEADS)], axis=-1)
            for h in range(WKV_HEADS)]
    return jnp.concatenate(rows, axis=-2)


def _pairs_to_state(bd):
    b = bd.shape[0]
    blocks = [bd[:, :, h * HEAD_SIZE:(h + 1) * HEAD_SIZE, h * HEAD_SIZE:(h + 1) * HEAD_SIZE]
              for h in range(WKV_HEADS)]
    return jnp.stack(blocks, axis=2).reshape(b, N_HEADS, HEAD_SIZE, HEAD_SIZE)


_PAIR_TABLE = ((0, 1), (0, 2), (0, 3), (1, 2), (1, 3), (2, 3))


def _moe_plan(counts, n_tiles, tm):
    cnt = counts[:N_CLASSES].astype(i32)
    tiles_per = (cnt + tm - 1) // tm
    tile_end = jnp.cumsum(tiles_per)
    tile_start = tile_end - tiles_per
    offsets = tile_start * tm
    tidx = jnp.arange(n_tiles, dtype=i32)
    cls_of_tile = jnp.sum((tidx[:, None] >= tile_end[None, :]).astype(i32), axis=1)
    valid = (cls_of_tile < N_CLASSES).astype(i32)
    last_cls = jnp.max(jnp.where(tiles_per > 0, jnp.arange(N_CLASSES, dtype=i32), 0))
    cls_c = jnp.where(valid > 0, cls_of_tile, last_cls)
    grp = cls_c // 6
    pr = cls_c % 6
    lo_tab = jnp.array([p[0] for p in _PAIR_TABLE], i32)
    hi_tab = jnp.array([p[1] for p in _PAIR_TABLE], i32)
    ea = grp * EXPERTS_PER_GROUP + lo_tab[pr]
    eb = grp * EXPERTS_PER_GROUP + hi_tab[pr]
    return offsets, ea, eb, valid


def _positions(info, offsets):
    cls = info[:, 0, :].reshape(-1)
    rank = info[:, 1, :].reshape(-1)
    return offsets[cls] + rank


def _router_weights(wg, bg, we, be):
    w = jnp.concatenate([wg, we], axis=1).T
    w = jnp.pad(w, ((0, ROUTER_ROWS - w.shape[0]), (0, 0)))
    hi, lo = _split2_outside(w)
    bias = jnp.pad(jnp.concatenate([bg, be]), (0, ROUTER_ROWS - N_GROUPS - N_EXPERTS))
    return jnp.concatenate([hi, lo], axis=0), bias.reshape(ROUTER_ROWS, 1)


def _tiles(bsz, seq, rows):
    tt = min(seq, rows)
    bs = max(rows // tt, 1)
    assert bsz % bs == 0 and seq % tt == 0
    return bs, tt


def kernel(x_prompt, x_sample, c_prompt, c_sample, state_conv, state_shift, state_wkv, ada_w, ada_b, norm_g, final_g, cv_in, cv_w, cv_out, rw_mu, rw_r, rw_k, rw_v, rw_o, rw_w0, rw_w1, rw_w2, rw_a0, rw_a1, rw_a2, rw_g1, rw_g2, rw_kk, rw_ka, rw_rk, rw_lnw, rw_lnb, moe_wg, moe_bg, moe_we, moe_be, moe_w1, moe_w3, moe_w2):
    d = D_MODEL
    xs = [x_prompt, x_sample]
    bszs = [x.shape[0] for x in xs]
    seqs = [x.shape[1] for x in xs]
    ntok = [b * s for b, s in zip(bszs, seqs)]
    n_total = sum(ntok)

    c_all = jnp.concatenate([c_prompt, c_sample], axis=0)
    rows_c = c_all.shape[0]
    rows_pad = -(-rows_c // 8) * 8
    mod = _ada(jnp.pad(c_all, ((0, rows_pad - rows_c), (0, 0))), ada_w, ada_b)

    def mods(layer, trunk):
        lo = 0 if trunk == 0 else bszs[0]
        m = mod[layer, lo:lo + bszs[trunk]]
        return [jnp.broadcast_to(m[:, None, i * d:(i + 1) * d], (bszs[trunk], SUBLANES, d))
                for i in range(N_ADA)]

    conv_states = [jnp.zeros((1, bszs[0], CONV_WIDTH - 1, d), f32), state_conv]
    shift_states = [jnp.zeros((1, bszs[0], d), f32), state_shift]
    wkv_states = [jnp.zeros((1, bszs[0], N_HEADS, HEAD_SIZE, HEAD_SIZE), f32), state_wkv]

    tri_cache = {}

    def tri(n):
        if n not in tri_cache:
            tri_cache[n] = (jnp.arange(n)[:, None] < jnp.arange(n)[None, :]).astype(bf16)
        return tri_cache[n]

    n_tiles = n_total // MOE_TILE + N_CLASSES
    p_rows = n_tiles * MOE_TILE
    expert_w = [w.reshape((-1,) + w.shape[2:]).astype(bf16) for w in (moe_w1, moe_w3, moe_w2)]

    def moe_layer(layer, rows_list, info_list, counts):
        offsets, ea, eb, valid = _moe_plan(counts[:, 0], n_tiles, MOE_TILE)
        pos_list = [_positions(info, offsets) for info in info_list]
        sorted_rows = jnp.zeros((p_rows, ROW_WORDS), u32)
        for rows, pos in zip(rows_list, pos_list):
            sorted_rows = _dispatch(rows, pos.reshape(-1, 1, MOE_TILE), sorted_rows, tm=MOE_TILE)
        first = layer * N_EXPERTS
        ys = _moe(sorted_rows, ea + first, eb + first, valid, *expert_w, tm=MOE_TILE)
        return ys, pos_list

    def out_router_pair(layer, acts, w_bf, xcur, tails=(None, None)):
        wrt, rbias = _router_weights(moe_wg[layer], moe_bg[layer], moe_we[layer], moe_be[layer])
        counts = jnp.zeros((CLASS_ROWS, 128), f32)
        x_new, rows_list, info_list = [], [], []
        for trunk in range(2):
            _, _, gt_m, sh_f, sc_f, _ = mods(layer, trunk)
            bs, tt = _tiles(bszs[trunk], seqs[trunk], ROUTER_TILE)
            xo, rows, info, counts = _out_router(
                acts[trunk], w_bf, xcur[trunk], gt_m, norm_g[layer, 1].reshape(1, d), sh_f, sc_f,
                wrt, rbias, counts, tri(bs * tt // ROUTER_SPLIT), bs=bs, tt=tt, tail=tails[trunk])
            x_new.append(xo)
            rows_list.append(rows)
            info_list.append(info)
        return x_new, rows_list, info_list, counts

    def combine_pair(layer, ys, pos_list, xcur, final):
        out = []
        for trunk in range(2):
            gt_f = mods(layer, trunk)[5]
            bs, tt = _tiles(bszs[trunk], seqs[trunk], COMBINE_TILE)
            out.append(_combine(ys, pos_list[trunk].reshape(-1, 1, bs * tt), xcur[trunk], gt_f,
                                final_g.reshape(1, d), bs=bs, tt=tt, final_norm=final))
        return out

    w_in_hi, w_in_lo = _split2_outside(cv_in[0])
    w_out_hi, w_out_lo = _split2_outside(cv_out[0])
    gain0 = norm_g[0, 0].reshape(1, d)
    conv_taps = jnp.repeat(cv_w[0], SUBLANES, axis=0)
    z0, conv_out = [], []
    for trunk in range(2):
        sh_m, sc_m = mods(0, trunk)[:2]
        bs, tt = _tiles(bszs[trunk], seqs[trunk], CONV_TILE)
        st8 = jnp.pad(conv_states[trunk][0], ((0, 0), (CARRY_ROWS - (CONV_WIDTH - 1), 0), (0, 0)))
        z, so = _conv_in(xs[trunk], sh_m, sc_m, gain0, cv_in[0].astype(bf16), conv_taps, st8, bs=bs, tt=tt)
        z0.append(z)
        conv_out.append(so[:, -1, CARRY_ROWS - (CONV_WIDTH - 1):, :][None])
    assert min(seqs) >= TAIL_WINDOW
    x_tail = jnp.concatenate([x[:, -TAIL_WINDOW:, :] for x in xs], axis=0)
    n_seq = x_tail.shape[0]
    mod_tail = [jnp.concatenate([mods(0, trunk)[i] for trunk in range(2)], axis=0) for i in range(3)]
    z_tail, _ = _conv_in(x_tail, mod_tail[0], mod_tail[1], gain0, w_in_hi, conv_taps,
                         jnp.zeros((n_seq, CARRY_ROWS, d), f32), bs=n_seq, tt=TAIL_WINDOW, w_in_lo=w_in_lo)
    x1_tail = _tail_out(z_tail, x_tail, mod_tail[2], w_out_hi, w_out_lo)[:, TAIL_WINDOW - TAIL_ROWS:, :]
    tails = (x1_tail[:bszs[0]], x1_tail[bszs[0]:])
    x1, rows0, info0, counts0 = out_router_pair(0, z0, cv_out[0].astype(bf16), xs, tails)
    ys0, pos0 = moe_layer(0, rows0, info0, counts0)
    x2 = combine_pair(0, ys0, pos0, x1, False)

    pad_l = ((0, 0), (0, LORA_PAD - rw_w1.shape[2]))
    pad_r = ((0, LORA_PAD - rw_w2.shape[1]), (0, 0))
    w1p = jnp.pad(rw_w1[0], pad_l).astype(bf16)
    a1p = jnp.pad(rw_a1[0], pad_l).astype(bf16)
    w2p = jnp.pad(rw_w2[0], pad_r).astype(bf16)
    a2p = jnp.pad(rw_a2[0], pad_r).astype(bf16)
    vec_p = jnp.repeat(jnp.stack([rw_w0[0], rw_a0[0], rw_kk[0], rw_ka[0]]), SUBLANES, axis=0)
    vec_w = jnp.repeat(jnp.stack([rw_lnw[0], rw_lnb[0], rw_rk[0].reshape(d)]), SUBLANES, axis=0)
    wr_bf, wk_bf, wv_bf = rw_r[0].astype(bf16), rw_k[0].astype(bf16), rw_v[0].astype(bf16)
    g1_bf, g2_bf = rw_g1[0].astype(bf16), rw_g2[0].astype(bf16)
    proj_tn = 256
    tril = (jnp.arange(WKV_CHUNK)[:, None] >= jnp.arange(WKV_CHUNK)[None, :]).astype(bf16)
    z1, shift_out, wkv_out = [], [], []
    for trunk in range(2):
        sh_m, sc_m = mods(1, trunk)[:2]
        bs, tt = _tiles(bszs[trunk], seqs[trunk], PROJ_TILE)
        r, k, v, kk, beta, g, lw, sho = _rwkv_proj(
            x2[trunk], sh_m, sc_m, norm_g[1, 0].reshape(1, d),
            jnp.broadcast_to(rw_mu[0][:, None, :], (N_SHIFT_MIX, SUBLANES, d)),
            shift_states[trunk][0].reshape(bszs[trunk], 1, d), wr_bf, wk_bf, wv_bf, w1p, a1p, g1_bf,
            w2p, a2p, g2_bf, vec_p, _block_ones(proj_tn), bs=bs, tt=tt, tn=proj_tn)
        tc = min(seqs[trunk], WKV_CHUNKS_PER_STEP * WKV_CHUNK)
        z, s_bd = _wkv(r, k, v, kk, beta, g, lw, _state_to_pairs(wkv_states[trunk][0]), vec_w, tril,
                       _block_ones(2 * PAIR), tc=tc)
        z1.append(z)
        shift_out.append(sho.reshape(1, bszs[trunk], d))
        wkv_out.append(_pairs_to_state(s_bd)[None])
    x3, rows1, info1, counts1 = out_router_pair(1, z1, rw_o[0].astype(bf16), x2)
    ys1, pos1 = moe_layer(1, rows1, info1, counts1)
    y = combine_pair(1, ys1, pos1, x3, True)

    return (y[0], y[1], conv_out[0], shift_out[0], wkv_out[0], conv_out[1], shift_out[1], wkv_out[1])
```

```python
import functools

import jax
import jax.numpy as jnp
from jax import lax
from jax.experimental import pallas as pl
from jax.experimental.pallas import tpu as pltpu

f32 = jnp.float32
bf16 = jnp.bfloat16
i32 = jnp.int32
u32 = jnp.uint32

D_MODEL = 2048
LANES = 128
SUBLANES = 8
HEAD_SIZE = 64
N_HEADS = D_MODEL // HEAD_SIZE
WKV_HEADS = 2
PAIR = WKV_HEADS * HEAD_SIZE
N_PAIRS = D_MODEL // PAIR
CONV_WIDTH = 3
N_GROUPS = 4
EXPERTS_PER_GROUP = 4
N_EXPERTS = N_GROUPS * EXPERTS_PER_GROUP
D_EXPERT = D_MODEL // 4
N_ADA = 6
N_SHIFT_MIX = 6
RMS_EPS = 1e-6
GN_EPS = 64e-5
LORA_PAD = 128
N_CLASSES = N_GROUPS * 6
CLASS_ROWS = 32
ROUTER_ROWS = 128
HALF = D_MODEL // 2
ROW_WORDS = HALF + 128
WKV_CHUNK = 64
WKV_CHUNKS_PER_STEP = 4
CARRY_ROWS = 8
TAIL_ROWS = 8
TAIL_WINDOW = 2 * TAIL_ROWS

V7X_VMEM_LIMIT = 56 * 1024 * 1024
MOE_TILE = 512
DMA_UNROLL = 16
ROUTER_TILE = 512
ROUTER_SPLIT = 1
COMBINE_TILE = 512
PROJ_TILE = 512
PROJ_COLS = 256
CONV_TILE = 1024


def _cparams(sem, vmem=V7X_VMEM_LIMIT):
    return pltpu.CompilerParams(dimension_semantics=sem, vmem_limit_bytes=vmem)


def _dot(a, b):
    return jnp.dot(a, b, preferred_element_type=f32)


def _dot_nt(a, b):
    return lax.dot_general(a, b, (((1,), (1,)), ((), ())), preferred_element_type=f32)


def _split2(x):
    hi = pltpu.bitcast(pltpu.bitcast(x, u32) & jnp.uint32(0xFFFF0000), f32)
    return hi.astype(bf16), (x - hi).astype(bf16)


def _split2_outside(x):
    hi = lax.bitcast_convert_type(lax.bitcast_convert_type(x, u32) & jnp.uint32(0xFFFF0000), f32)
    return hi.astype(bf16), (x - hi).astype(bf16)


def _sigmoid(x):
    return 1.0 / (1.0 + jnp.exp(-x))


def _pack_rows(x):
    hi = pltpu.bitcast(x[:, :HALF].astype(bf16).astype(f32), u32)
    lo = pltpu.bitcast(x[:, HALF:].astype(bf16).astype(f32), u32)
    return (hi & jnp.uint32(0xFFFF0000)) | (lo >> 16)


def _unpack_rows(p):
    left = pltpu.bitcast(p & jnp.uint32(0xFFFF0000), f32)
    right = pltpu.bitcast(p << 16, f32)
    return left, right


def _dot_split(a_hi, a_lo, w_hi, w_lo):
    return (_dot(a_hi, w_hi) + _dot(a_lo, w_hi)) + (_dot(a_hi, w_lo) + _dot(a_lo, w_lo))


def _ada_body(c_ref, w_ref, b_ref, o_ref):
    c = c_ref[...]
    s_hi, s_lo = _split2(c * _sigmoid(c))
    w_hi, w_lo = _split2(w_ref[0])
    o_ref[0] = _dot_split(s_hi, s_lo, w_hi, w_lo) + b_ref[0]


def _ada(c_all, ada_w, ada_b, tn=1024):
    depth, d, n = ada_w.shape
    rows = c_all.shape[0]
    return pl.pallas_call(
        _ada_body,
        grid=(depth, n // tn),
        in_specs=[
            pl.BlockSpec((rows, d), lambda l, j: (0, 0)),
            pl.BlockSpec((1, d, tn), lambda l, j: (l, 0, j)),
            pl.BlockSpec((1, 1, tn), lambda l, j: (l, 0, j)),
        ],
        out_specs=pl.BlockSpec((1, rows, tn), lambda l, j: (l, 0, j)),
        out_shape=jax.ShapeDtypeStruct((depth, rows, n), f32),
        compiler_params=_cparams(("arbitrary", "arbitrary")),
        name="ada",
    )(c_all, ada_w, ada_b.reshape(depth, 1, n))


def _inv_rms(x2):
    sq = x2 * x2
    part = sq[:, 0:LANES]
    for i in range(1, D_MODEL // LANES):
        part = part + sq[:, i * LANES:(i + 1) * LANES]
    hi, lo = _split2(part)
    ones = jnp.ones((LANES, LANES), bf16)
    ms = (_dot(hi, ones) + _dot(lo, ones)) * (1.0 / D_MODEL)
    rs = lax.rsqrt(ms + RMS_EPS)
    return jnp.concatenate([rs] * (D_MODEL // LANES), axis=1)


def _per_seq(x, fn, *vecs):
    bs, tt, n = x.shape
    x4 = x.reshape(bs, tt // SUBLANES, SUBLANES, n)
    return fn(x4, *(v[:, None] for v in vecs)).reshape(bs, tt, n)


def _gated_add(x, gate, y):
    return _per_seq(y, lambda y4, g4: g4 * y4, gate) + x


def _norm_mod(x, gain, sh, sc):
    bs, tt, d = x.shape
    x2 = x.reshape(bs * tt, d)
    y = (x2 * _inv_rms(x2) * gain).reshape(bs, tt, d)
    return _per_seq(y, lambda y4, sh4, sc4: y4 * (1.0 + sc4) + sh4, sh, sc)


def _conv_in_body(x_ref, sh_ref, sc_ref, g_ref, cw_ref, st_ref, *rest, bs, tt, tn, precise):
    nw = 6 if precise else 3
    w_refs = rest[:nw]
    z_ref, so_ref, h_scr, carry_scr, uext_scr = rest[nw:]
    t = pl.program_id(1)
    j = pl.program_id(2)

    @pl.when(j == 0)
    def _():
        h = _norm_mod(x_ref[...], g_ref[...], sh_ref[...], sc_ref[...]).reshape(bs * tt, D_MODEL)
        if precise:
            h_scr[0], h_scr[1] = _split2(h)
        else:
            h_scr[0] = h.astype(bf16)

    @pl.when(t == 0)
    def _():
        carry_scr[j] = st_ref[...]

    if precise:
        b_gate, c_gate, xv = (_dot_split(h_scr[0], h_scr[1], w_refs[i][...], w_refs[i + 3][...]) for i in range(3))
    else:
        b_gate, c_gate, xv = (_dot(h_scr[0], w_refs[i][...]) for i in range(3))
    u = (c_gate * xv).reshape(bs, tt, tn)
    uext_scr[:, 0:CARRY_ROWS, :] = carry_scr[j]
    uext_scr[:, CARRY_ROWS:, :] = u
    cw = cw_ref[...]
    tap = lambda i: cw[i * SUBLANES:(i + 1) * SUBLANES]
    tiles = lambda x: x.reshape(bs, tt // SUBLANES, SUBLANES, tn)
    conv = (tiles(uext_scr[:, CARRY_ROWS - 2:CARRY_ROWS - 2 + tt, :]) * tap(0)
            + tiles(uext_scr[:, CARRY_ROWS - 1:CARRY_ROWS - 1 + tt, :]) * tap(1)
            + tiles(u) * tap(2))
    z_ref[...] = (b_gate.reshape(bs, tt, tn) * conv.reshape(bs, tt, tn)).astype(z_ref.dtype)
    last = uext_scr[:, tt:tt + CARRY_ROWS, :]
    carry_scr[j] = last
    so_ref[:, 0] = last


def _conv_in(x, sh, sc, gain, w_in_hi, conv_w, state8, *, bs, tt, tn=512, w_in_lo=None):
    bsz, seq, d = x.shape
    nj = d // tn
    precise = w_in_lo is not None
    body = functools.partial(_conv_in_body, bs=bs, tt=tt, tn=tn, precise=precise)
    w_specs = [pl.BlockSpec((d, tn), lambda b, t, j, k=k: (0, j + k * nj)) for k in range(3)]
    weights = [w_in_hi] * 3 + ([w_in_lo] * 3 if precise else [])
    return pl.pallas_call(
        body,
        grid=(bsz // bs, seq // tt, nj),
        in_specs=[
            pl.BlockSpec((bs, tt, d), lambda b, t, j: (b, t, 0)),
            pl.BlockSpec((bs, SUBLANES, d), lambda b, t, j: (b, 0, 0)),
            pl.BlockSpec((bs, SUBLANES, d), lambda b, t, j: (b, 0, 0)),
            pl.BlockSpec((1, d), lambda b, t, j: (0, 0)),
            pl.BlockSpec((CONV_WIDTH * SUBLANES, tn), lambda b, t, j: (0, j)),
            pl.BlockSpec((bs, CARRY_ROWS, tn), lambda b, t, j: (b, 0, j)),
        ] + w_specs * (2 if precise else 1),
        out_specs=[
            pl.BlockSpec((bs, tt, tn), lambda b, t, j: (b, t, j)),
            pl.BlockSpec((bs, 1, CARRY_ROWS, tn), lambda b, t, j: (b, t, 0, j)),
        ],
        out_shape=[
            jax.ShapeDtypeStruct((bsz, seq, d), f32 if precise else bf16),
            jax.ShapeDtypeStruct((bsz, seq // tt, CARRY_ROWS, d), f32),
        ],
        scratch_shapes=[
            pltpu.VMEM((2 if precise else 1, bs * tt, d), bf16),
            pltpu.VMEM((nj, bs, CARRY_ROWS, tn), f32),
            pltpu.VMEM((bs, tt + CARRY_ROWS, tn), f32),
        ],
        compiler_params=_cparams(("arbitrary", "arbitrary", "arbitrary")),
        name="conv_in_precise" if precise else "conv_in",
    )(x, sh, sc, gain, conv_w, state8, *weights)


def _tail_out_body(z_ref, x_ref, gt_ref, wh_ref, wl_ref, o_ref, *, bs, tt):
    z_hi, z_lo = _split2(z_ref[...].reshape(bs * tt, D_MODEL))
    y = _dot_split(z_hi, z_lo, wh_ref[...], wl_ref[...])
    tn = y.shape[1]
    o_ref[...] = _gated_add(x_ref[...], gt_ref[...], y.reshape(bs, tt, tn))


def _tail_out(z, x, gt, w_hi, w_lo, *, tn=512):
    bsz, tt, d = x.shape
    body = functools.partial(_tail_out_body, bs=bsz, tt=tt)
    return pl.pallas_call(
        body,
        grid=(d // tn,),
        in_specs=[
            pl.BlockSpec((bsz, tt, d), lambda j: (0, 0, 0)),
            pl.BlockSpec((bsz, tt, tn), lambda j: (0, 0, j)),
            pl.BlockSpec((bsz, SUBLANES, tn), lambda j: (0, 0, j)),
            pl.BlockSpec((d, tn), lambda j: (0, j)),
            pl.BlockSpec((d, tn), lambda j: (0, j)),
        ],
        out_specs=pl.BlockSpec((bsz, tt, tn), lambda j: (0, 0, j)),
        out_shape=jax.ShapeDtypeStruct((bsz, tt, d), f32),
        compiler_params=_cparams(("arbitrary",)),
        name="tail_out",
    )(z, x, gt, w_hi, w_lo)


def _route_rows(logit):
    lg = [logit[g:g + 1, :] for g in range(N_GROUPS)]
    le = [logit[N_GROUPS + e:N_GROUPS + e + 1, :] for e in range(N_EXPERTS)]
    gmax = jnp.maximum(jnp.maximum(lg[0], lg[1]), jnp.maximum(lg[2], lg[3]))
    gidx = jnp.where(lg[0] == gmax, 0, jnp.where(lg[1] == gmax, 1, jnp.where(lg[2] == gmax, 2, 3)))
    denom = (jnp.exp(lg[0] - gmax) + jnp.exp(lg[1] - gmax)
             + jnp.exp(lg[2] - gmax) + jnp.exp(lg[3] - gmax))
    p_grp = 1.0 / denom
    leg = [jnp.where(gidx == 0, le[i],
                     jnp.where(gidx == 1, le[EXPERTS_PER_GROUP + i],
                               jnp.where(gidx == 2, le[2 * EXPERTS_PER_GROUP + i],
                                         le[3 * EXPERTS_PER_GROUP + i])))
           for i in range(EXPERTS_PER_GROUP)]
    v1 = jnp.maximum(jnp.maximum(leg[0], leg[1]), jnp.maximum(leg[2], leg[3]))
    i1 = jnp.where(leg[0] == v1, 0, jnp.where(leg[1] == v1, 1, jnp.where(leg[2] == v1, 2, 3)))
    neg = jnp.float32(-jnp.inf)
    rest = [jnp.where(i1 == i, neg, leg[i]) for i in range(EXPERTS_PER_GROUP)]
    v2 = jnp.maximum(jnp.maximum(rest[0], rest[1]), jnp.maximum(rest[2], rest[3]))
    i2 = jnp.where((rest[0] == v2) & (i1 != 0), 0,
                   jnp.where((rest[1] == v2) & (i1 != 1), 1,
                             jnp.where((rest[2] == v2) & (i1 != 2), 2, 3)))
    s = jnp.exp(v2 - v1)
    w_first = p_grp / (1.0 + s)
    w_second = p_grp * s / (1.0 + s)
    i_lo = jnp.minimum(i1, i2)
    i_hi = jnp.maximum(i1, i2)
    pair_base = jnp.where(i_lo == 0, 0, jnp.where(i_lo == 1, 3, 5))
    cls = gidx * 6 + pair_base + (i_hi - i_lo - 1)
    g_lo = jnp.where(i1 < i2, w_first, w_second)
    g_hi = jnp.where(i1 < i2, w_second, w_first)
    return cls, g_lo, g_hi


def _out_router_body(a_ref, w_ref, x_ref, gt_ref, g_ref, sh_ref, sc_ref, wrt_ref, rb_ref,
                     cin_ref, tri_ref, *rest, bs, tt, with_tail):
    if with_tail:
        tail_ref, xo_ref, rows_ref, info_ref, cout_ref, cnt_scr = rest
    else:
        xo_ref, rows_ref, info_ref, cout_ref, cnt_scr = rest
    first = (pl.program_id(0) == 0) & (pl.program_id(1) == 0)

    @pl.when(first)
    def _():
        cnt_scr[...] = cin_ref[...]

    is_last_t = pl.program_id(1) == pl.num_programs(1) - 1
    split_seqs = bs >= ROUTER_SPLIT
    hb = bs // ROUTER_SPLIT if split_seqs else bs
    ht = tt if split_seqs else tt // ROUTER_SPLIT
    hm = hb * ht
    wrt = wrt_ref[...]
    crow = lax.broadcasted_iota(i32, (CLASS_ROWS, hm), 0)
    irow = lax.broadcasted_iota(i32, (8, hm), 0)
    grow = lax.broadcasted_iota(i32, (ROUTER_ROWS, hm), 0)
    base = cnt_scr[:, 0:1]
    for part in range(ROUTER_SPLIT):
        bsl = slice(part * hb, (part + 1) * hb) if split_seqs else slice(None)
        tsl = slice(None) if split_seqs else slice(part * ht, (part + 1) * ht)
        rsl = slice(part * hm, (part + 1) * hm)
        y = _dot(a_ref[bsl, tsl, :].reshape(hm, D_MODEL), w_ref[...])
        xn = _gated_add(x_ref[bsl, tsl, :], gt_ref[bsl], y.reshape(hb, ht, D_MODEL))
        if with_tail and (split_seqs or part == ROUTER_SPLIT - 1):
            patched = jnp.concatenate([xn[:, :ht - TAIL_ROWS], tail_ref[bsl]], axis=1)
            xn = jnp.where(is_last_t, patched, xn)
        xo_ref[bsl, tsl, :] = xn
        h = _norm_mod(xn, g_ref[...], sh_ref[bsl], sc_ref[bsl]).reshape(hm, D_MODEL)
        h_hi, h_lo = _split2(h)
        p_hi = _dot_nt(wrt, h_hi)
        p_lo = _dot_nt(wrt, h_lo)
        logit = ((p_hi[:ROUTER_ROWS] + p_hi[ROUTER_ROWS:]) + (p_lo[:ROUTER_ROWS] + p_lo[ROUTER_ROWS:])
                 + rb_ref[...])
        cls, g_lo, g_hi = _route_rows(logit)

        onehot = (crow == cls).astype(f32)
        before = _dot(onehot.astype(bf16), tri_ref[...])
        rank = jnp.sum(onehot * (before + base), axis=0, keepdims=True).astype(i32)
        base = base + jnp.sum(onehot, axis=1, keepdims=True)
        info_ref[0, :, rsl] = jnp.where(irow == 0, cls, jnp.where(irow == 1, rank, 0))

        gates_t = jnp.where(grow == 0, g_lo, jnp.where(grow == 1, g_hi, 0.0))
        rows_ref[rsl, :HALF] = _pack_rows(h)
        rows_ref[rsl, HALF:] = pltpu.bitcast(gates_t.T, u32)
    cnt_new = jnp.broadcast_to(base, cnt_scr.shape)
    cnt_scr[...] = cnt_new
    cout_ref[...] = cnt_new


def _out_router(a, w_bf, x, gt, gain, sh, sc, wrt, rbias, cnt_in, tri, *, bs, tt, tail=None):
    bsz, seq, d = x.shape
    tm = bs * tt
    nt = seq // tt
    ntiles = (bsz // bs) * nt
    with_tail = tail is not None
    body = functools.partial(_out_router_body, bs=bs, tt=tt, with_tail=with_tail)
    tail_specs = [pl.BlockSpec((bs, TAIL_ROWS, d), lambda b, t: (b, 0, 0))] if with_tail else []
    tail_args = [tail] if with_tail else []
    return pl.pallas_call(
        body,
        grid=(bsz // bs, nt),
        in_specs=[
            pl.BlockSpec((bs, tt, d), lambda b, t: (b, t, 0)),
            pl.BlockSpec((d, d), lambda b, t: (0, 0), pipeline_mode=pl.Buffered(1)),
            pl.BlockSpec((bs, tt, d), lambda b, t: (b, t, 0)),
            pl.BlockSpec((bs, SUBLANES, d), lambda b, t: (b, 0, 0)),
            pl.BlockSpec((1, d), lambda b, t: (0, 0)),
            pl.BlockSpec((bs, SUBLANES, d), lambda b, t: (b, 0, 0)),
            pl.BlockSpec((bs, SUBLANES, d), lambda b, t: (b, 0, 0)),
            pl.BlockSpec((2 * ROUTER_ROWS, d), lambda b, t: (0, 0)),
            pl.BlockSpec((ROUTER_ROWS, 1), lambda b, t: (0, 0)),
            pl.BlockSpec((CLASS_ROWS, 128), lambda b, t: (0, 0)),
            pl.BlockSpec((tm // ROUTER_SPLIT, tm // ROUTER_SPLIT), lambda b, t: (0, 0)),
        ] + tail_specs,
        out_specs=[
            pl.BlockSpec((bs, tt, d), lambda b, t: (b, t, 0)),
            pl.BlockSpec((tm, ROW_WORDS), lambda b, t: (b * nt + t, 0)),
            pl.BlockSpec((1, 8, tm), lambda b, t: (b * nt + t, 0, 0)),
            pl.BlockSpec((CLASS_ROWS, 128), lambda b, t: (0, 0)),
        ],
        out_shape=[
            jax.ShapeDtypeStruct((bsz, seq, d), f32),
            jax.ShapeDtypeStruct((bsz * seq, ROW_WORDS), u32),
            jax.ShapeDtypeStruct((ntiles, 8, tm), i32),
            jax.ShapeDtypeStruct((CLASS_ROWS, 128), f32),
        ],
        scratch_shapes=[pltpu.VMEM((CLASS_ROWS, 128), f32)],
        compiler_params=_cparams(("arbitrary", "arbitrary")),
        name="out_router",
    )(a, w_bf, x, gt, gain, sh, sc, wrt, rbias, cnt_in, tri, *tail_args)


def _row_copy_scatter(src_ref, dst_ref, pos_ref, sem, i, k):
    row = i * SUBLANES + k
    return pltpu.make_async_copy(src_ref.at[i, pl.ds(k, 1), :], dst_ref.at[pl.ds(pos_ref[0, 0, row], 1), :], sem)


def _for_each_row(tm, fn):
    def tile(i, carry):
        for k in range(SUBLANES):
            fn(i, k)
        return carry

    lax.fori_loop(0, tm // SUBLANES, tile, 0, unroll=DMA_UNROLL // SUBLANES)


def _dispatch_body(pos_ref, rows_ref, dst_in_ref, dst_ref, sem, *, tm):
    del dst_in_ref
    _for_each_row(tm, lambda i, k: _row_copy_scatter(rows_ref, dst_ref, pos_ref, sem, i, k).start())
    _for_each_row(tm, lambda i, k: _row_copy_scatter(rows_ref, dst_ref, pos_ref, sem, i, k).wait())


def _dispatch(rows, pos3, sorted_rows, *, tm):
    n = rows.shape[0] * SUBLANES
    body = functools.partial(_dispatch_body, tm=tm)
    return pl.pallas_call(
        body,
        grid=(n // tm,),
        in_specs=[
            pl.BlockSpec((1, 1, tm), lambda i: (i, 0, 0), memory_space=pltpu.SMEM),
            pl.BlockSpec((tm // SUBLANES, SUBLANES, ROW_WORDS), lambda i: (i, 0, 0)),
            pl.BlockSpec(memory_space=pl.ANY),
        ],
        out_specs=pl.BlockSpec(memory_space=pl.ANY),
        out_shape=jax.ShapeDtypeStruct(sorted_rows.shape, sorted_rows.dtype),
        scratch_shapes=[pltpu.SemaphoreType.DMA(())],
        input_output_aliases={2: 0},
        compiler_params=_cparams(("arbitrary",)),
        name="moe_dispatch",
    )(pos3, rows, sorted_rows)


def _moe_body(ea_ref, eb_ref, valid_ref, xs_ref, w1a_ref, w3a_ref, w2a_ref, w1b_ref, w3b_ref, w2b_ref,
              ys_ref):
    del ea_ref, eb_ref
    i = pl.program_id(0)

    @pl.when(valid_ref[i] > 0)
    def _():
        left, right = _unpack_rows(xs_ref[:, :HALF])
        x = jnp.concatenate([left.astype(bf16), right.astype(bf16)], axis=1)
        gates = pltpu.bitcast(xs_ref[:, HALF:], f32)

        def expert(w1_ref, w3_ref, w2_ref, gate):
            h1 = _dot(x, w1_ref[0])
            h3 = _dot(x, w3_ref[0])
            hid = (h1 * _sigmoid(h1)) * h3 * gate
            return _dot(hid.astype(bf16), w2_ref[0])

        y = expert(w1a_ref, w3a_ref, w2a_ref, gates[:, 0:1]) + expert(w1b_ref, w3b_ref, w2b_ref, gates[:, 1:2])
        ys_ref[...] = _pack_rows(y)

    @pl.when(valid_ref[i] == 0)
    def _():
        ys_ref[...] = jnp.zeros(ys_ref.shape, u32)


def _moe(sorted_rows, tile_ea, tile_eb, tile_valid, w1_bf, w3_bf, w2_bf, *, tm):
    p = sorted_rows.shape[0]
    d, f = D_MODEL, D_EXPERT

    def wa(i, ea, eb, valid):
        return (ea[i], 0, 0)

    def wb(i, ea, eb, valid):
        return (eb[i], 0, 0)

    grid_spec = pltpu.PrefetchScalarGridSpec(
        num_scalar_prefetch=3,
        grid=(p // tm,),
        in_specs=[
            pl.BlockSpec((tm, ROW_WORDS), lambda i, ea, eb, valid: (i, 0)),
            pl.BlockSpec((1, d, f), wa), pl.BlockSpec((1, d, f), wa), pl.BlockSpec((1, f, d), wa),
            pl.BlockSpec((1, d, f), wb), pl.BlockSpec((1, d, f), wb), pl.BlockSpec((1, f, d), wb),
        ],
        out_specs=pl.BlockSpec((tm, HALF), lambda i, ea, eb, valid: (i, 0)),
    )
    return pl.pallas_call(
        _moe_body,
        grid_spec=grid_spec,
        out_shape=jax.ShapeDtypeStruct((p, HALF), u32),
        compiler_params=_cparams(("arbitrary",)),
        name="moe_experts",
    )(tile_ea, tile_eb, tile_valid, sorted_rows, w1_bf, w3_bf, w2_bf, w1_bf, w3_bf, w2_bf)


def _row_copy_gather(src_ref, dst_ref, pos_ref, sem, i, k):
    row = i * SUBLANES + k
    return pltpu.make_async_copy(src_ref.at[pl.ds(pos_ref[0, 0, row], 1), :], dst_ref.at[i, pl.ds(k, 1), :], sem)


def _combine_body(pos_ref, pos_next_ref, ys_ref, x_ref, gt_ref, fg_ref, o_ref, buf, sems, *, bs, tt, final_norm):
    tm = bs * tt
    nt = pl.num_programs(1)
    step = pl.program_id(0) * nt + pl.program_id(1)
    nsteps = pl.num_programs(0) * nt
    slot = step % 2

    def issue(p_ref, s):
        _for_each_row(tm, lambda i, k: _row_copy_gather(ys_ref, buf.at[s], p_ref, sems.at[s], i, k).start())

    @pl.when(step == 0)
    def _():
        issue(pos_ref, 0)

    @pl.when(step + 1 < nsteps)
    def _():
        issue(pos_next_ref, 1 - slot)

    _for_each_row(tm, lambda i, k: _row_copy_gather(ys_ref, buf.at[slot], pos_ref, sems.at[slot], i, k).wait())

    left, right = _unpack_rows(buf[slot].reshape(tm, HALF))
    y = jnp.concatenate([left, right], axis=1).reshape(bs, tt, D_MODEL)
    xn = _gated_add(x_ref[...], gt_ref[...], y)
    if final_norm:
        x2 = xn.reshape(tm, D_MODEL)
        xn = (x2 * _inv_rms(x2) * fg_ref[...]).reshape(bs, tt, D_MODEL)
    o_ref[...] = xn


def _combine(ys, pos3, x, gt, final_g, *, bs, tt, final_norm):
    bsz, seq, d = x.shape
    tm = bs * tt
    nt = seq // tt
    nsteps = (bsz // bs) * nt
    body = functools.partial(_combine_body, bs=bs, tt=tt, final_norm=final_norm)
    return pl.pallas_call(
        body,
        grid=(bsz // bs, nt),
        in_specs=[
            pl.BlockSpec((1, 1, tm), lambda b, t: (b * nt + t, 0, 0), memory_space=pltpu.SMEM),
            pl.BlockSpec((1, 1, tm), lambda b, t: (jnp.minimum(b * nt + t + 1, nsteps - 1), 0, 0),
                         memory_space=pltpu.SMEM),
            pl.BlockSpec(memory_space=pl.ANY),
            pl.BlockSpec((bs, tt, d), lambda b, t: (b, t, 0)),
            pl.BlockSpec((bs, SUBLANES, d), lambda b, t: (b, 0, 0)),
            pl.BlockSpec((1, d), lambda b, t: (0, 0)),
        ],
        out_specs=pl.BlockSpec((bs, tt, d), lambda b, t: (b, t, 0)),
        out_shape=jax.ShapeDtypeStruct((bsz, seq, d), f32),
        scratch_shapes=[pltpu.VMEM((2, tm // SUBLANES, SUBLANES, HALF), u32), pltpu.SemaphoreType.DMA((2,))],
        compiler_params=_cparams(("arbitrary", "arbitrary")),
        name="moe_combine",
    )(pos3, pos3, ys, x, gt, final_g)


def _rwkv_proj_body(x_ref, sh_ref, sc_ref, g_ref, mu_ref, shift_ref,
                    wr_ref, wk_ref, wv_ref, w1_ref, a1_ref, g1_ref, w2_ref, a2_ref, g2_ref,
                    vec_ref, ones_ref,
                    r_o, k_o, v_o, kk_o, b_o, g_o, lw_o, sh_o,
                    hs_scr, mix_scr, l1w_scr, l1a_scr, l1g_scr, *, bs, tt, tn):
    tm = bs * tt
    t = pl.program_id(1)
    j = pl.program_id(2)
    prev_row = CARRY_ROWS - 1

    @pl.when(j == 0)
    def _():
        @pl.when(t == 0)
        def _():
            hs_scr[:, prev_row:CARRY_ROWS, :] = shift_ref[...]

        h = _norm_mod(x_ref[...], g_ref[...], sh_ref[...], sc_ref[...])
        hs_scr[:, CARRY_ROWS:, :] = h
        h_prev = hs_scr[:, prev_row:prev_row + tt, :]
        h4 = h.reshape(bs, tt // SUBLANES, SUBLANES, D_MODEL)
        xx4 = h_prev.reshape(h4.shape) - h4
        for m in range(N_SHIFT_MIX):
            mix_scr[m] = (h4 + xx4 * mu_ref[m]).reshape(tm, D_MODEL).astype(bf16)
        last = hs_scr[:, prev_row + tt:CARRY_ROWS + tt, :]
        hs_scr[:, prev_row:CARRY_ROWS, :] = last
        sh_o[...] = last
        l1w_scr[...] = jnp.tanh(_dot(mix_scr[1], w1_ref[...])).astype(bf16)
        l1a_scr[...] = _dot(mix_scr[4], a1_ref[...]).astype(bf16)
        l1g_scr[...] = _sigmoid(_dot(mix_scr[5], g1_ref[...])).astype(bf16)

    vec = vec_ref[...]
    w0, a0, k_k, k_a = (vec[i * SUBLANES:(i + 1) * SUBLANES] for i in range(4))
    tiles = lambda x: x.reshape(tm // SUBLANES, SUBLANES, tn)
    r = _dot(mix_scr[0], wr_ref[...])
    k = tiles(_dot(mix_scr[2], wk_ref[...]))
    v = _dot(mix_scr[3], wv_ref[...])
    wl = w0 + tiles(_dot(l1w_scr[...], w2_ref[...]))
    a = _sigmoid(a0 + tiles(_dot(l1a_scr[...], a2_ref[...])))
    g = _dot(l1g_scr[...], g2_ref[...])
    neg = -wl
    softplus = jnp.maximum(neg, 0.0) + jnp.log(1.0 + jnp.exp(-jnp.abs(neg)))
    w_log = -softplus - 0.5
    lw = -jnp.exp(w_log)
    kkr = k * k_k
    ss = tiles(_dot((kkr * kkr).reshape(tm, tn).astype(bf16), ones_ref[...]))
    kk = kkr * lax.rsqrt(jnp.maximum(ss, 1e-24))
    k2 = k * (1.0 + (a - 1.0) * k_a)
    shp = (bs, tt, tn)
    r_o[...] = r.reshape(shp).astype(bf16)
    k_o[...] = k2.reshape(shp).astype(bf16)
    v_o[...] = v.reshape(shp).astype(bf16)
    kk_o[...] = kk.reshape(shp).astype(bf16)
    b_o[...] = (kk * a).reshape(shp).astype(bf16)
    g_o[...] = g.reshape(shp).astype(bf16)
    lw_o[...] = lw.reshape(shp)


def _rwkv_proj(x, sh, sc, gain, mu, shift, wr, wk, wv, w1p, a1p, g1, w2p, a2p, g2, vec, ones_bd,
               *, bs, tt, tn=256):
    bsz, seq, d = x.shape
    tm = bs * tt
    dg = g1.shape[1]
    body = functools.partial(_rwkv_proj_body, bs=bs, tt=tt, tn=tn)
    const2 = lambda b, t, j: (0, 0)
    colblk = lambda b, t, j: (0, j)
    tok = lambda b, t, j: (b, t, j)
    act = lambda dt: jax.ShapeDtypeStruct((bsz, seq, d), dt)
    return pl.pallas_call(
        body,
        grid=(bsz // bs, seq // tt, d // tn),
        in_specs=[
            pl.BlockSpec((bs, tt, d), lambda b, t, j: (b, t, 0)),
            pl.BlockSpec((bs, SUBLANES, d), lambda b, t, j: (b, 0, 0)),
            pl.BlockSpec((bs, SUBLANES, d), lambda b, t, j: (b, 0, 0)),
            pl.BlockSpec((1, d), const2),
            pl.BlockSpec((N_SHIFT_MIX, SUBLANES, d), lambda b, t, j: (0, 0, 0)),
            pl.BlockSpec((bs, 1, d), lambda b, t, j: (b, 0, 0)),
            pl.BlockSpec((d, tn), colblk), pl.BlockSpec((d, tn), colblk), pl.BlockSpec((d, tn), colblk),
            pl.BlockSpec((d, LORA_PAD), const2), pl.BlockSpec((d, LORA_PAD), const2),
            pl.BlockSpec((d, dg), const2),
            pl.BlockSpec((LORA_PAD, tn), colblk), pl.BlockSpec((LORA_PAD, tn), colblk),
            pl.BlockSpec((dg, tn), colblk),
            pl.BlockSpec((4 * SUBLANES, tn), colblk),
            pl.BlockSpec((tn, tn), const2),
        ],
        out_specs=[pl.BlockSpec((bs, tt, tn), tok)] * 7 + [pl.BlockSpec((bs, 1, d), lambda b, t, j: (b, 0, 0))],
        out_shape=[act(bf16)] * 6 + [act(f32), jax.ShapeDtypeStruct((bsz, 1, d), f32)],
        scratch_shapes=[
            pltpu.VMEM((bs, tt + CARRY_ROWS, d), f32),
            pltpu.VMEM((N_SHIFT_MIX, tm, d), bf16),
            pltpu.VMEM((tm, LORA_PAD), bf16),
            pltpu.VMEM((tm, LORA_PAD), bf16),
            pltpu.VMEM((tm, dg), bf16),
        ],
        compiler_params=_cparams(("arbitrary", "arbitrary", "arbitrary")),
        name="rwkv_proj",
    )(x, sh, sc, gain, mu, shift, wr, wk, wv, w1p, a1p, g1, w2p, a2p, g2, vec, ones_bd)


def _block_diag_rows(x, heads):
    return jnp.concatenate([jnp.where(m, x, 0.0) for m in heads], axis=1).astype(bf16)


def _bmm(a, b):
    return lax.dot_general(a, b, (((2,), (1,)), ((0,), (0,))), preferred_element_type=f32)


def _bmm_nt(a, b):
    return lax.dot_general(a, b, (((2,), (2,)), ((0,), (0,))), preferred_element_type=f32)


def _wkv_chunk(r, k, v, kk, beta, lw, c, c_end, state, masks):
    heads, strict, incl, eye, bd_mask = masks
    n = WKV_CHUNK
    bd = lambda x: _block_diag_rows(x, heads)
    g_inv = jnp.exp(-c)
    g_end = jnp.exp(c_end - c)
    a_t = -(kk * jnp.exp(c - lw))
    r_t = r * jnp.exp(c)
    ar = jnp.concatenate([a_t, r_t], axis=1).astype(bf16)
    a12 = _bmm_nt(ar, jnp.concatenate([bd(beta * g_inv), bd(k * g_inv)], axis=1))
    a_ab = jnp.where(strict, a12[:, :n, :PAIR], 0.0)
    a_rb = jnp.where(incl, a12[:, n:, :PAIR], 0.0)
    a_ak = jnp.where(strict, a12[:, :n, PAIR:], 0.0)
    a_rk = jnp.where(incl, a12[:, n:, PAIR:], 0.0)
    from_v = _bmm(jnp.concatenate([a_ak, a_rk], axis=1).astype(bf16), bd(v))
    av, ov = from_v[:, :n], from_v[:, n:]
    q = a_ab
    p = eye + q
    q = _bmm(q.astype(bf16), bd(q))
    for level in range(5):
        if level < 4:
            res = _bmm(jnp.concatenate([p, q], axis=1).astype(bf16), bd(q))
            p = p + res[:, :n]
            q = res[:, n:]
        else:
            p = p + _bmm(p.astype(bf16), bd(q))
    p_bf = p.astype(bf16)
    both = _bmm(p_bf, jnp.concatenate([bd(a_t), bd(av)], axis=2))
    a_bar, u0 = both[:, :, :PAIR], both[:, :, PAIR:]
    s_bf = state.astype(bf16)
    from_state = _bmm_nt(jnp.concatenate([a_bar, r_t], axis=1).astype(bf16), s_bf)
    u = from_state[:, :n] + u0
    o = from_state[:, n:] + ov + _bmm(a_rb.astype(bf16), bd(u))
    uv_t = jnp.swapaxes(jnp.concatenate([u, v], axis=1), 1, 2).astype(bf16)
    bk = jnp.concatenate([beta * g_end, k * g_end], axis=1).astype(bf16)
    decay = jnp.exp(c_end)
    new_state = (state * jnp.concatenate([decay] * WKV_HEADS, axis=1)
                 + jnp.where(bd_mask, _bmm(uv_t, bk), 0.0))
    return o, new_state


def _wkv_body(r_ref, k_ref, v_ref, kk_ref, b_ref, g_ref, lw_ref, s0_ref, vec_ref, tril_ref, ones_ref,
              z_ref, so_ref, s_scr, *, tc):
    t = pl.program_id(1)
    n = WKV_CHUNK

    @pl.when(t == 0)
    def _():
        s_scr[...] = s0_ref[0]

    lane = lax.broadcasted_iota(i32, (n, PAIR), 1)
    row = lax.broadcasted_iota(i32, (n, PAIR), 0)
    head_of = lambda idx: jnp.right_shift(idx, HEAD_SIZE.bit_length() - 1)
    heads = [head_of(lane) == h for h in range(WKV_HEADS)]
    within = lane & (HEAD_SIZE - 1)
    strict = within < row
    incl = within <= row
    eye = (within == row).astype(f32)
    bd_mask = (head_of(lax.broadcasted_iota(i32, (PAIR, PAIR), 0))
               == head_of(lax.broadcasted_iota(i32, (PAIR, PAIR), 1)))
    masks = (heads, strict, incl, eye, bd_mask)
    tril = tril_ref[...]
    ones_bd = ones_ref[...]
    rows = min(tc, n)
    nchunks = max(tc // n, 1)

    def units(x):
        return jnp.stack([x[:, p * PAIR:(p + 1) * PAIR] for p in range(N_PAIRS)], axis=0)

    def seg_sum(x):
        wide = jnp.concatenate([jnp.concatenate([x[2 * i], x[2 * i + 1]], axis=1)
                                for i in range(N_PAIRS // 2)], axis=0)
        s = _dot(wide.astype(bf16), ones_bd)
        return jnp.stack([s[(p // 2) * n:(p // 2 + 1) * n, (p % 2) * PAIR:(p % 2 + 1) * PAIR]
                          for p in range(N_PAIRS)], axis=0)

    vec = units(vec_ref[...])
    ln_w, ln_b, r_k = (vec[:, i * SUBLANES:(i + 1) * SUBLANES] for i in range(3))

    def load(ref, ci):
        x = ref[0, ci * n:ci * n + rows, :].astype(f32)
        if rows < n:
            x = jnp.concatenate([x, jnp.zeros((n - rows, x.shape[1]), f32)], axis=0)
        return x

    for ci in range(nchunks):
        lw = load(lw_ref, ci)
        hi, lo = _split2(lw)
        sums = _dot(tril, hi) + _dot(tril, lo)
        total = jnp.broadcast_to(sums[n - 1:n, :], sums.shape)
        r, k, v = (units(load(ref, ci)) for ref in (r_ref, k_ref, v_ref))
        o, s_new = _wkv_chunk(r, k, v, units(load(kk_ref, ci)), units(load(b_ref, ci)), units(lw),
                              units(sums), units(total), s_scr[...], masks)
        s_scr[...] = s_new
        mean = seg_sum(o) * (1.0 / HEAD_SIZE)
        dev = o - mean
        var = seg_sum(dev * dev) * (1.0 / HEAD_SIZE)
        y = _per_seq(dev * lax.rsqrt(var + GN_EPS), lambda y4, w4, b4: y4 * w4 + b4, ln_w, ln_b)
        y = y + seg_sum(_per_seq(r * k, lambda x4, rk4: x4 * rk4, r_k)) * v
        z = (y * units(load(g_ref, ci))).astype(bf16)
        z_ref[0, ci * n:ci * n + rows, :] = jnp.concatenate([z[p, :rows] for p in range(N_PAIRS)], axis=1)

    @pl.when(t == pl.num_programs(1) - 1)
    def _():
        so_ref[0] = s_scr[...]


def _wkv(r, k, v, kk, beta, g, lw, state_bd, vec, tril, ones_bd, *, tc):
    bsz, seq, d = r.shape
    body = functools.partial(_wkv_body, tc=tc)
    tok = pl.BlockSpec((1, tc, d), lambda b, t: (b, t, 0))
    st = pl.BlockSpec((1, N_PAIRS, PAIR, PAIR), lambda b, t: (b, 0, 0, 0))
    return pl.pallas_call(
        body,
        grid=(bsz, seq // tc),
        in_specs=[tok] * 7 + [
            st,
            pl.BlockSpec((3 * SUBLANES, d), lambda b, t: (0, 0)),
            pl.BlockSpec((WKV_CHUNK, WKV_CHUNK), lambda b, t: (0, 0)),
            pl.BlockSpec((2 * PAIR, 2 * PAIR), lambda b, t: (0, 0)),
        ],
        out_specs=[tok, st],
        out_shape=[jax.ShapeDtypeStruct((bsz, seq, d), bf16),
                   jax.ShapeDtypeStruct(state_bd.shape, f32)],
        scratch_shapes=[pltpu.VMEM((N_PAIRS, PAIR, PAIR), f32)],
        compiler_params=_cparams(("arbitrary", "arbitrary")),
        name="wkv",
    )(r, k, v, kk, beta, g, lw, state_bd, vec, tril, ones_bd)


def _block_ones(n):
    idx = jnp.arange(n) // HEAD_SIZE
    return (idx[:, None] == idx[None, :]).astype(bf16)


def _state_to_pairs(s):
    b = s.shape[0]
    s5 = s.reshape(b, N_PAIRS, WKV_HEADS, HEAD_SIZE, HEAD_SIZE)
    zero = jnp.zeros((b, N_PAIRS, HEAD_SIZE, HEAD_SIZE), s.dtype)
    rows = [jnp.concatenate([s5[:, :, h] if j == h else zero for j in range(WKV_HEADS)], axis=-1)
            for h in range(WKV_HEADS)]
    return jnp.concatenate(rows, axis=-2)


def _pairs_to_state(bd):
    b = bd.shape[0]
    blocks = [bd[:, :, h * HEAD_SIZE:(h + 1) * HEAD_SIZE, h * HEAD_SIZE:(h + 1) * HEAD_SIZE]
              for h in range(WKV_HEADS)]
    return jnp.stack(blocks, axis=2).reshape(b, N_HEADS, HEAD_SIZE, HEAD_SIZE)


_PAIR_TABLE = ((0, 1), (0, 2), (0, 3), (1, 2), (1, 3), (2, 3))


def _moe_plan(counts, n_tiles, tm):
    cnt = counts[:N_CLASSES].astype(i32)
    tiles_per = (cnt + tm - 1) // tm
    tile_end = jnp.cumsum(tiles_per)
    tile_start = tile_end - tiles_per
    offsets = tile_start * tm
    tidx = jnp.arange(n_tiles, dtype=i32)
    cls_of_tile = jnp.sum((tidx[:, None] >= tile_end[None, :]).astype(i32), axis=1)
    valid = (cls_of_tile < N_CLASSES).astype(i32)
    last_cls = jnp.max(jnp.where(tiles_per > 0, jnp.arange(N_CLASSES, dtype=i32), 0))
    cls_c = jnp.where(valid > 0, cls_of_tile, last_cls)
    grp = cls_c // 6
    pr = cls_c % 6
    lo_tab = jnp.array([p[0] for p in _PAIR_TABLE], i32)
    hi_tab = jnp.array([p[1] for p in _PAIR_TABLE], i32)
    ea = grp * EXPERTS_PER_GROUP + lo_tab[pr]
    eb = grp * EXPERTS_PER_GROUP + hi_tab[pr]
    return offsets, ea, eb, valid


def _positions(info, offsets):
    cls = info[:, 0, :].reshape(-1)
    rank = info[:, 1, :].reshape(-1)
    return offsets[cls] + rank


def _router_weights(wg, bg, we, be):
    w = jnp.concatenate([wg, we], axis=1).T
    w = jnp.pad(w, ((0, ROUTER_ROWS - w.shape[0]), (0, 0)))
    hi, lo = _split2_outside(w)
    bias = jnp.pad(jnp.concatenate([bg, be]), (0, ROUTER_ROWS - N_GROUPS - N_EXPERTS))
    return jnp.concatenate([hi, lo], axis=0), bias.reshape(ROUTER_ROWS, 1)


def _tiles(bsz, seq, rows):
    tt = min(seq, rows)
    bs = max(rows // tt, 1)
    assert bsz % bs == 0 and seq % tt == 0
    return bs, tt


def kernel(x_prompt, x_sample, c_prompt, c_sample, state_conv, state_shift, state_wkv, ada_w, ada_b, norm_g, final_g, cv_in, cv_w, cv_out, rw_mu, rw_r, rw_k, rw_v, rw_o, rw_w0, rw_w1, rw_w2, rw_a0, rw_a1, rw_a2, rw_g1, rw_g2, rw_kk, rw_ka, rw_rk, rw_lnw, rw_lnb, moe_wg, moe_bg, moe_we, moe_be, moe_w1, moe_w3, moe_w2):
    d = D_MODEL
    xs = [x_prompt, x_sample]
    bszs = [x.shape[0] for x in xs]
    seqs = [x.shape[1] for x in xs]
    ntok = [b * s for b, s in zip(bszs, seqs)]
    n_total = sum(ntok)

    c_all = jnp.concatenate([c_prompt, c_sample], axis=0)
    rows_c = c_all.shape[0]
    rows_pad = -(-rows_c // 8) * 8
    mod = _ada(jnp.pad(c_all, ((0, rows_pad - rows_c), (0, 0))), ada_w, ada_b)

    def mods(layer, trunk):
        lo = 0 if trunk == 0 else bszs[0]
        m = mod[layer, lo:lo + bszs[trunk]]
        return [jnp.broadcast_to(m[:, None, i * d:(i + 1) * d], (bszs[trunk], SUBLANES, d))
                for i in range(N_ADA)]

    conv_states = [jnp.zeros((1, bszs[0], CONV_WIDTH - 1, d), f32), state_conv]
    shift_states = [jnp.zeros((1, bszs[0], d), f32), state_shift]
    wkv_states = [jnp.zeros((1, bszs[0], N_HEADS, HEAD_SIZE, HEAD_SIZE), f32), state_wkv]

    tri_cache = {}

    def tri(n):
        if n not in tri_cache:
            tri_cache[n] = (jnp.arange(n)[:, None] < jnp.arange(n)[None, :]).astype(bf16)
        return tri_cache[n]

    n_tiles = n_total // MOE_TILE + N_CLASSES
    p_rows = n_tiles * MOE_TILE
    expert_w = [w.reshape((-1,) + w.shape[2:]).astype(bf16) for w in (moe_w1, moe_w3, moe_w2)]

    def moe_layer(layer, rows_list, info_list, counts):
        offsets, ea, eb, valid = _moe_plan(counts[:, 0], n_tiles, MOE_TILE)
        pos_list = [_positions(info, offsets) for info in info_list]
        sorted_rows = jnp.zeros((p_rows, ROW_WORDS), u32)
        for rows, pos in zip(rows_list, pos_list):
            sorted_rows = _dispatch(rows.reshape(-1, SUBLANES, ROW_WORDS), pos.reshape(-1, 1, MOE_TILE),
                                    sorted_rows, tm=MOE_TILE)
        first = layer * N_EXPERTS
        ys = _moe(sorted_rows, ea + first, eb + first, valid, *expert_w, tm=MOE_TILE)
        return ys, pos_list

    def out_router_pair(layer, acts, w_bf, xcur, tails=(None, None)):
        wrt, rbias = _router_weights(moe_wg[layer], moe_bg[layer], moe_we[layer], moe_be[layer])
        counts = jnp.zeros((CLASS_ROWS, 128), f32)
        x_new, rows_list, info_list = [], [], []
        for trunk in range(2):
            _, _, gt_m, sh_f, sc_f, _ = mods(layer, trunk)
            bs, tt = _tiles(bszs[trunk], seqs[trunk], ROUTER_TILE)
            xo, rows, info, counts = _out_router(
                acts[trunk], w_bf, xcur[trunk], gt_m, norm_g[layer, 1].reshape(1, d), sh_f, sc_f,
                wrt, rbias, counts, tri(bs * tt // ROUTER_SPLIT), bs=bs, tt=tt, tail=tails[trunk])
            x_new.append(xo)
            rows_list.append(rows)
            info_list.append(info)
        return x_new, rows_list, info_list, counts

    def combine_pair(layer, ys, pos_list, xcur, final):
        out = []
        for trunk in range(2):
            gt_f = mods(layer, trunk)[5]
            bs, tt = _tiles(bszs[trunk], seqs[trunk], COMBINE_TILE)
            out.append(_combine(ys, pos_list[trunk].reshape(-1, 1, bs * tt), xcur[trunk], gt_f,
                                final_g.reshape(1, d), bs=bs, tt=tt, final_norm=final))
        return out

    w_in_hi, w_in_lo = _split2_outside(cv_in[0])
    w_out_hi, w_out_lo = _split2_outside(cv_out[0])
    gain0 = norm_g[0, 0].reshape(1, d)
    conv_taps = jnp.repeat(cv_w[0], SUBLANES, axis=0)
    z0, conv_out = [], []
    for trunk in range(2):
        sh_m, sc_m = mods(0, trunk)[:2]
        bs, tt = _tiles(bszs[trunk], seqs[trunk], CONV_TILE)
        st8 = jnp.pad(conv_states[trunk][0], ((0, 0), (CARRY_ROWS - (CONV_WIDTH - 1), 0), (0, 0)))
        z, so = _conv_in(xs[trunk], sh_m, sc_m, gain0, cv_in[0].astype(bf16), conv_taps, st8, bs=bs, tt=tt)
        z0.append(z)
        conv_out.append(so[:, -1, CARRY_ROWS - (CONV_WIDTH - 1):, :][None])
    assert min(seqs) >= TAIL_WINDOW
    x_tail = jnp.concatenate([x[:, -TAIL_WINDOW:, :] for x in xs], axis=0)
    n_seq = x_tail.shape[0]
    mod_tail = [jnp.concatenate([mods(0, trunk)[i] for trunk in range(2)], axis=0) for i in range(3)]
    z_tail, _ = _conv_in(x_tail, mod_tail[0], mod_tail[1], gain0, w_in_hi, conv_taps,
                         jnp.zeros((n_seq, CARRY_ROWS, d), f32), bs=n_seq, tt=TAIL_WINDOW, w_in_lo=w_in_lo)
    x1_tail = _tail_out(z_tail, x_tail, mod_tail[2], w_out_hi, w_out_lo)[:, TAIL_WINDOW - TAIL_ROWS:, :]
    tails = (x1_tail[:bszs[0]], x1_tail[bszs[0]:])
    x1, rows0, info0, counts0 = out_router_pair(0, z0, cv_out[0].astype(bf16), xs, tails)
    ys0, pos0 = moe_layer(0, rows0, info0, counts0)
    x2 = combine_pair(0, ys0, pos0, x1, False)

    pad_l = ((0, 0), (0, LORA_PAD - rw_w1.shape[2]))
    pad_r = ((0, LORA_PAD - rw_w2.shape[1]), (0, 0))
    w1p = jnp.pad(rw_w1[0], pad_l).astype(bf16)
    a1p = jnp.pad(rw_a1[0], pad_l).astype(bf16)
    w2p = jnp.pad(rw_w2[0], pad_r).astype(bf16)
    a2p = jnp.pad(rw_a2[0], pad_r).astype(bf16)
    vec_p = jnp.repeat(jnp.stack([rw_w0[0], rw_a0[0], rw_kk[0], rw_ka[0]]), SUBLANES, axis=0)
    vec_w = jnp.repeat(jnp.stack([rw_lnw[0], rw_lnb[0], rw_rk[0].reshape(d)]), SUBLANES, axis=0)
    wr_bf, wk_bf, wv_bf = rw_r[0].astype(bf16), rw_k[0].astype(bf16), rw_v[0].astype(bf16)
    g1_bf, g2_bf = rw_g1[0].astype(bf16), rw_g2[0].astype(bf16)
    proj_tn = PROJ_COLS
    tril = (jnp.arange(WKV_CHUNK)[:, None] >= jnp.arange(WKV_CHUNK)[None, :]).astype(bf16)
    z1, shift_out, wkv_out = [], [], []
    for trunk in range(2):
        sh_m, sc_m = mods(1, trunk)[:2]
        bs, tt = _tiles(bszs[trunk], seqs[trunk], PROJ_TILE)
        r, k, v, kk, beta, g, lw, sho = _rwkv_proj(
            x2[trunk], sh_m, sc_m, norm_g[1, 0].reshape(1, d),
            jnp.broadcast_to(rw_mu[0][:, None, :], (N_SHIFT_MIX, SUBLANES, d)),
            shift_states[trunk][0].reshape(bszs[trunk], 1, d), wr_bf, wk_bf, wv_bf, w1p, a1p, g1_bf,
            w2p, a2p, g2_bf, vec_p, _block_ones(proj_tn), bs=bs, tt=tt, tn=proj_tn)
        tc = min(seqs[trunk], WKV_CHUNKS_PER_STEP * WKV_CHUNK)
        z, s_bd = _wkv(r, k, v, kk, beta, g, lw, _state_to_pairs(wkv_states[trunk][0]), vec_w, tril,
                       _block_ones(2 * PAIR), tc=tc)
        z1.append(z)
        shift_out.append(sho.reshape(1, bszs[trunk], d))
        wkv_out.append(_pairs_to_state(s_bd)[None])
    x3, rows1, info1, counts1 = out_router_pair(1, z1, rw_o[0].astype(bf16), x2)
    ys1, pos1 = moe_layer(1, rows1, info1, counts1)
    y = combine_pair(1, ys1, pos1, x3, True)

    return (y[0], y[1], conv_out[0], shift_out[0], wkv_out[0], conv_out[1], shift_out[1], wkv_out[1])
```

```python
import functools

import jax
import jax.numpy as jnp
from jax import lax
from jax.experimental import pallas as pl
from jax.experimental.pallas import tpu as pltpu

f32 = jnp.float32
bf16 = jnp.bfloat16
i32 = jnp.int32
u32 = jnp.uint32

D_MODEL = 2048
LANES = 128
SUBLANES = 8
HEAD_SIZE = 64
N_HEADS = D_MODEL // HEAD_SIZE
WKV_HEADS = 2
PAIR = WKV_HEADS * HEAD_SIZE
N_PAIRS = D_MODEL // PAIR
CONV_WIDTH = 3
N_GROUPS = 4
EXPERTS_PER_GROUP = 4
N_EXPERTS = N_GROUPS * EXPERTS_PER_GROUP
D_EXPERT = D_MODEL // 4
N_ADA = 6
N_SHIFT_MIX = 6
RMS_EPS = 1e-6
GN_EPS = 64e-5
LORA_PAD = 128
N_CLASSES = N_GROUPS * 6
CLASS_ROWS = 32
ROUTER_ROWS = 128
HALF = D_MODEL // 2
ROW_WORDS = HALF + 128
WKV_CHUNK = 64
WKV_CHUNKS_PER_STEP = 4
CARRY_ROWS = 8
TAIL_ROWS = 8
TAIL_WINDOW = 2 * TAIL_ROWS

V7X_VMEM_LIMIT = 56 * 1024 * 1024
MOE_TILE = 512
DMA_UNROLL = 16
ROUTER_TILE = 512
ROUTER_SPLIT = 1
COMBINE_TILE = 512
PROJ_TILE = 512
PROJ_COLS = 256
CONV_TILE = 1024


def _cparams(sem, vmem=V7X_VMEM_LIMIT):
    return pltpu.CompilerParams(dimension_semantics=sem, vmem_limit_bytes=vmem)


def _dot(a, b):
    return jnp.dot(a, b, preferred_element_type=f32)


def _dot_nt(a, b):
    return lax.dot_general(a, b, (((1,), (1,)), ((), ())), preferred_element_type=f32)


def _split2(x):
    hi = pltpu.bitcast(pltpu.bitcast(x, u32) & jnp.uint32(0xFFFF0000), f32)
    return hi.astype(bf16), (x - hi).astype(bf16)


def _split2_outside(x):
    hi = lax.bitcast_convert_type(lax.bitcast_convert_type(x, u32) & jnp.uint32(0xFFFF0000), f32)
    return hi.astype(bf16), (x - hi).astype(bf16)


def _sigmoid(x):
    return 1.0 / (1.0 + jnp.exp(-x))


def _pack_rows(x):
    hi = pltpu.bitcast(x[:, :HALF].astype(bf16).astype(f32), u32)
    lo = pltpu.bitcast(x[:, HALF:].astype(bf16).astype(f32), u32)
    return (hi & jnp.uint32(0xFFFF0000)) | (lo >> 16)


def _unpack_rows(p):
    left = pltpu.bitcast(p & jnp.uint32(0xFFFF0000), f32)
    right = pltpu.bitcast(p << 16, f32)
    return left, right


def _dot_split(a_hi, a_lo, w_hi, w_lo):
    return (_dot(a_hi, w_hi) + _dot(a_lo, w_hi)) + (_dot(a_hi, w_lo) + _dot(a_lo, w_lo))


def _ada_body(c_ref, w_ref, b_ref, o_ref):
    c = c_ref[...]
    s_hi, s_lo = _split2(c * _sigmoid(c))
    w_hi, w_lo = _split2(w_ref[0])
    o_ref[0] = _dot_split(s_hi, s_lo, w_hi, w_lo) + b_ref[0]


def _ada(c_all, ada_w, ada_b, tn=1024):
    depth, d, n = ada_w.shape
    rows = c_all.shape[0]
    return pl.pallas_call(
        _ada_body,
        grid=(depth, n // tn),
        in_specs=[
            pl.BlockSpec((rows, d), lambda l, j: (0, 0)),
            pl.BlockSpec((1, d, tn), lambda l, j: (l, 0, j)),
            pl.BlockSpec((1, 1, tn), lambda l, j: (l, 0, j)),
        ],
        out_specs=pl.BlockSpec((1, rows, tn), lambda l, j: (l, 0, j)),
        out_shape=jax.ShapeDtypeStruct((depth, rows, n), f32),
        compiler_params=_cparams(("arbitrary", "arbitrary")),
        name="ada",
    )(c_all, ada_w, ada_b.reshape(depth, 1, n))


def _inv_rms(x2):
    sq = x2 * x2
    part = sq[:, 0:LANES]
    for i in range(1, D_MODEL // LANES):
        part = part + sq[:, i * LANES:(i + 1) * LANES]
    hi, lo = _split2(part)
    ones = jnp.ones((LANES, LANES), bf16)
    ms = (_dot(hi, ones) + _dot(lo, ones)) * (1.0 / D_MODEL)
    rs = lax.rsqrt(ms + RMS_EPS)
    return jnp.concatenate([rs] * (D_MODEL // LANES), axis=1)


def _per_seq(x, fn, *vecs):
    bs, tt, n = x.shape
    x4 = x.reshape(bs, tt // SUBLANES, SUBLANES, n)
    return fn(x4, *(v[:, None] for v in vecs)).reshape(bs, tt, n)


def _gated_add(x, gate, y):
    return _per_seq(y, lambda y4, g4: g4 * y4, gate) + x


def _norm_mod(x, gain, sh, sc):
    bs, tt, d = x.shape
    x2 = x.reshape(bs * tt, d)
    y = (x2 * _inv_rms(x2) * gain).reshape(bs, tt, d)
    return _per_seq(y, lambda y4, sh4, sc4: y4 * (1.0 + sc4) + sh4, sh, sc)


def _conv_in_body(x_ref, sh_ref, sc_ref, g_ref, cw_ref, st_ref, *rest, bs, tt, tn, precise):
    nw = 6 if precise else 3
    w_refs = rest[:nw]
    z_ref, so_ref, h_scr, carry_scr, uext_scr = rest[nw:]
    t = pl.program_id(1)
    j = pl.program_id(2)

    @pl.when(j == 0)
    def _():
        h = _norm_mod(x_ref[...], g_ref[...], sh_ref[...], sc_ref[...]).reshape(bs * tt, D_MODEL)
        if precise:
            h_scr[0], h_scr[1] = _split2(h)
        else:
            h_scr[0] = h.astype(bf16)

    @pl.when(t == 0)
    def _():
        carry_scr[j] = st_ref[...]

    if precise:
        b_gate, c_gate, xv = (_dot_split(h_scr[0], h_scr[1], w_refs[i][...], w_refs[i + 3][...]) for i in range(3))
    else:
        b_gate, c_gate, xv = (_dot(h_scr[0], w_refs[i][...]) for i in range(3))
    u = (c_gate * xv).reshape(bs, tt, tn)
    uext_scr[:, 0:CARRY_ROWS, :] = carry_scr[j]
    uext_scr[:, CARRY_ROWS:, :] = u
    cw = cw_ref[...]
    tap = lambda i: cw[i * SUBLANES:(i + 1) * SUBLANES]
    tiles = lambda x: x.reshape(bs, tt // SUBLANES, SUBLANES, tn)
    conv = (tiles(uext_scr[:, CARRY_ROWS - 2:CARRY_ROWS - 2 + tt, :]) * tap(0)
            + tiles(uext_scr[:, CARRY_ROWS - 1:CARRY_ROWS - 1 + tt, :]) * tap(1)
            + tiles(u) * tap(2))
    z_ref[...] = (b_gate.reshape(bs, tt, tn) * conv.reshape(bs, tt, tn)).astype(z_ref.dtype)
    last = uext_scr[:, tt:tt + CARRY_ROWS, :]
    carry_scr[j] = last
    so_ref[:, 0] = last


def _conv_in(x, sh, sc, gain, w_in_hi, conv_w, state8, *, bs, tt, tn=512, w_in_lo=None):
    bsz, seq, d = x.shape
    nj = d // tn
    precise = w_in_lo is not None
    body = functools.partial(_conv_in_body, bs=bs, tt=tt, tn=tn, precise=precise)
    w_specs = [pl.BlockSpec((d, tn), lambda b, t, j, k=k: (0, j + k * nj)) for k in range(3)]
    weights = [w_in_hi] * 3 + ([w_in_lo] * 3 if precise else [])
    return pl.pallas_call(
        body,
        grid=(bsz // bs, seq // tt, nj),
        in_specs=[
            pl.BlockSpec((bs, tt, d), lambda b, t, j: (b, t, 0)),
            pl.BlockSpec((bs, SUBLANES, d), lambda b, t, j: (b, 0, 0)),
            pl.BlockSpec((bs, SUBLANES, d), lambda b, t, j: (b, 0, 0)),
            pl.BlockSpec((1, d), lambda b, t, j: (0, 0)),
            pl.BlockSpec((CONV_WIDTH * SUBLANES, tn), lambda b, t, j: (0, j)),
            pl.BlockSpec((bs, CARRY_ROWS, tn), lambda b, t, j: (b, 0, j)),
        ] + w_specs * (2 if precise else 1),
        out_specs=[
            pl.BlockSpec((bs, tt, tn), lambda b, t, j: (b, t, j)),
            pl.BlockSpec((bs, 1, CARRY_ROWS, tn), lambda b, t, j: (b, t, 0, j)),
        ],
        out_shape=[
            jax.ShapeDtypeStruct((bsz, seq, d), f32 if precise else bf16),
            jax.ShapeDtypeStruct((bsz, seq // tt, CARRY_ROWS, d), f32),
        ],
        scratch_shapes=[
            pltpu.VMEM((2 if precise else 1, bs * tt, d), bf16),
            pltpu.VMEM((nj, bs, CARRY_ROWS, tn), f32),
            pltpu.VMEM((bs, tt + CARRY_ROWS, tn), f32),
        ],
        compiler_params=_cparams(("arbitrary", "arbitrary", "arbitrary")),
        name="conv_in_precise" if precise else "conv_in",
    )(x, sh, sc, gain, conv_w, state8, *weights)


def _tail_out_body(z_ref, x_ref, gt_ref, wh_ref, wl_ref, o_ref, *, bs, tt):
    z_hi, z_lo = _split2(z_ref[...].reshape(bs * tt, D_MODEL))
    y = _dot_split(z_hi, z_lo, wh_ref[...], wl_ref[...])
    tn = y.shape[1]
    o_ref[...] = _gated_add(x_ref[...], gt_ref[...], y.reshape(bs, tt, tn))


def _tail_out(z, x, gt, w_hi, w_lo, *, tn=512):
    bsz, tt, d = x.shape
    body = functools.partial(_tail_out_body, bs=bsz, tt=tt)
    return pl.pallas_call(
        body,
        grid=(d // tn,),
        in_specs=[
            pl.BlockSpec((bsz, tt, d), lambda j: (0, 0, 0)),
            pl.BlockSpec((bsz, tt, tn), lambda j: (0, 0, j)),
            pl.BlockSpec((bsz, SUBLANES, tn), lambda j: (0, 0, j)),
            pl.BlockSpec((d, tn), lambda j: (0, j)),
            pl.BlockSpec((d, tn), lambda j: (0, j)),
        ],
        out_specs=pl.BlockSpec((bsz, tt, tn), lambda j: (0, 0, j)),
        out_shape=jax.ShapeDtypeStruct((bsz, tt, d), f32),
        compiler_params=_cparams(("arbitrary",)),
        name="tail_out",
    )(z, x, gt, w_hi, w_lo)


def _route_rows(logit):
    lg = [logit[g:g + 1, :] for g in range(N_GROUPS)]
    le = [logit[N_GROUPS + e:N_GROUPS + e + 1, :] for e in range(N_EXPERTS)]
    gmax = jnp.maximum(jnp.maximum(lg[0], lg[1]), jnp.maximum(lg[2], lg[3]))
    gidx = jnp.where(lg[0] == gmax, 0, jnp.where(lg[1] == gmax, 1, jnp.where(lg[2] == gmax, 2, 3)))
    denom = (jnp.exp(lg[0] - gmax) + jnp.exp(lg[1] - gmax)
             + jnp.exp(lg[2] - gmax) + jnp.exp(lg[3] - gmax))
    p_grp = 1.0 / denom
    leg = [jnp.where(gidx == 0, le[i],
                     jnp.where(gidx == 1, le[EXPERTS_PER_GROUP + i],
                               jnp.where(gidx == 2, le[2 * EXPERTS_PER_GROUP + i],
                                         le[3 * EXPERTS_PER_GROUP + i])))
           for i in range(EXPERTS_PER_GROUP)]
    v1 = jnp.maximum(jnp.maximum(leg[0], leg[1]), jnp.maximum(leg[2], leg[3]))
    i1 = jnp.where(leg[0] == v1, 0, jnp.where(leg[1] == v1, 1, jnp.where(leg[2] == v1, 2, 3)))
    neg = jnp.float32(-jnp.inf)
    rest = [jnp.where(i1 == i, neg, leg[i]) for i in range(EXPERTS_PER_GROUP)]
    v2 = jnp.maximum(jnp.maximum(rest[0], rest[1]), jnp.maximum(rest[2], rest[3]))
    i2 = jnp.where((rest[0] == v2) & (i1 != 0), 0,
                   jnp.where((rest[1] == v2) & (i1 != 1), 1,
                             jnp.where((rest[2] == v2) & (i1 != 2), 2, 3)))
    s = jnp.exp(v2 - v1)
    w_first = p_grp / (1.0 + s)
    w_second = p_grp * s / (1.0 + s)
    i_lo = jnp.minimum(i1, i2)
    i_hi = jnp.maximum(i1, i2)
    pair_base = jnp.where(i_lo == 0, 0, jnp.where(i_lo == 1, 3, 5))
    cls = gidx * 6 + pair_base + (i_hi - i_lo - 1)
    g_lo = jnp.where(i1 < i2, w_first, w_second)
    g_hi = jnp.where(i1 < i2, w_second, w_first)
    return cls, g_lo, g_hi


def _out_router_body(a_ref, w_ref, x_ref, gt_ref, g_ref, sh_ref, sc_ref, wrt_ref, rb_ref,
                     cin_ref, tri_ref, *rest, bs, tt, with_tail):
    if with_tail:
        tail_ref, xo_ref, rows_ref, info_ref, cout_ref, cnt_scr = rest
    else:
        xo_ref, rows_ref, info_ref, cout_ref, cnt_scr = rest
    first = (pl.program_id(0) == 0) & (pl.program_id(1) == 0)

    @pl.when(first)
    def _():
        cnt_scr[...] = cin_ref[...]

    is_last_t = pl.program_id(1) == pl.num_programs(1) - 1
    split_seqs = bs >= ROUTER_SPLIT
    hb = bs // ROUTER_SPLIT if split_seqs else bs
    ht = tt if split_seqs else tt // ROUTER_SPLIT
    hm = hb * ht
    wrt = wrt_ref[...]
    crow = lax.broadcasted_iota(i32, (CLASS_ROWS, hm), 0)
    irow = lax.broadcasted_iota(i32, (8, hm), 0)
    grow = lax.broadcasted_iota(i32, (ROUTER_ROWS, hm), 0)
    base = cnt_scr[:, 0:1]
    for part in range(ROUTER_SPLIT):
        bsl = slice(part * hb, (part + 1) * hb) if split_seqs else slice(None)
        tsl = slice(None) if split_seqs else slice(part * ht, (part + 1) * ht)
        rsl = slice(part * hm, (part + 1) * hm)
        y = _dot(a_ref[bsl, tsl, :].reshape(hm, D_MODEL), w_ref[...])
        xn = _gated_add(x_ref[bsl, tsl, :], gt_ref[bsl], y.reshape(hb, ht, D_MODEL))
        if with_tail and (split_seqs or part == ROUTER_SPLIT - 1):
            patched = jnp.concatenate([xn[:, :ht - TAIL_ROWS], tail_ref[bsl]], axis=1)
            xn = jnp.where(is_last_t, patched, xn)
        xo_ref[bsl, tsl, :] = xn
        h = _norm_mod(xn, g_ref[...], sh_ref[bsl], sc_ref[bsl]).reshape(hm, D_MODEL)
        h_hi, h_lo = _split2(h)
        p_hi = _dot_nt(wrt, h_hi)
        p_lo = _dot_nt(wrt, h_lo)
        logit = ((p_hi[:ROUTER_ROWS] + p_hi[ROUTER_ROWS:]) + (p_lo[:ROUTER_ROWS] + p_lo[ROUTER_ROWS:])
                 + rb_ref[...])
        cls, g_lo, g_hi = _route_rows(logit)

        onehot = (crow == cls).astype(f32)
        before = _dot(onehot.astype(bf16), tri_ref[...])
        rank = jnp.sum(onehot * (before + base), axis=0, keepdims=True).astype(i32)
        base = base + jnp.sum(onehot, axis=1, keepdims=True)
        info_ref[0, :, rsl] = jnp.where(irow == 0, cls, jnp.where(irow == 1, rank, 0))

        gates_t = jnp.where(grow == 0, g_lo, jnp.where(grow == 1, g_hi, 0.0))
        rows_ref[rsl, :HALF] = _pack_rows(h)
        rows_ref[rsl, HALF:] = pltpu.bitcast(gates_t.T, u32)
    cnt_new = jnp.broadcast_to(base, cnt_scr.shape)
    cnt_scr[...] = cnt_new
    cout_ref[...] = cnt_new


def _out_router(a, w_bf, x, gt, gain, sh, sc, wrt, rbias, cnt_in, tri, *, bs, tt, tail=None):
    bsz, seq, d = x.shape
    tm = bs * tt
    nt = seq // tt
    ntiles = (bsz // bs) * nt
    with_tail = tail is not None
    body = functools.partial(_out_router_body, bs=bs, tt=tt, with_tail=with_tail)
    tail_specs = [pl.BlockSpec((bs, TAIL_ROWS, d), lambda b, t: (b, 0, 0))] if with_tail else []
    tail_args = [tail] if with_tail else []
    return pl.pallas_call(
        body,
        grid=(bsz // bs, nt),
        in_specs=[
            pl.BlockSpec((bs, tt, d), lambda b, t: (b, t, 0)),
            pl.BlockSpec((d, d), lambda b, t: (0, 0), pipeline_mode=pl.Buffered(1)),
            pl.BlockSpec((bs, tt, d), lambda b, t: (b, t, 0)),
            pl.BlockSpec((bs, SUBLANES, d), lambda b, t: (b, 0, 0)),
            pl.BlockSpec((1, d), lambda b, t: (0, 0)),
            pl.BlockSpec((bs, SUBLANES, d), lambda b, t: (b, 0, 0)),
            pl.BlockSpec((bs, SUBLANES, d), lambda b, t: (b, 0, 0)),
            pl.BlockSpec((2 * ROUTER_ROWS, d), lambda b, t: (0, 0)),
            pl.BlockSpec((ROUTER_ROWS, 1), lambda b, t: (0, 0)),
            pl.BlockSpec((CLASS_ROWS, 128), lambda b, t: (0, 0)),
            pl.BlockSpec((tm // ROUTER_SPLIT, tm // ROUTER_SPLIT), lambda b, t: (0, 0)),
        ] + tail_specs,
        out_specs=[
            pl.BlockSpec((bs, tt, d), lambda b, t: (b, t, 0)),
            pl.BlockSpec((tm, ROW_WORDS), lambda b, t: (b * nt + t, 0)),
            pl.BlockSpec((1, 8, tm), lambda b, t: (b * nt + t, 0, 0)),
            pl.BlockSpec((CLASS_ROWS, 128), lambda b, t: (0, 0)),
        ],
        out_shape=[
            jax.ShapeDtypeStruct((bsz, seq, d), f32),
            jax.ShapeDtypeStruct((bsz * seq, ROW_WORDS), u32),
            jax.ShapeDtypeStruct((ntiles, 8, tm), i32),
            jax.ShapeDtypeStruct((CLASS_ROWS, 128), f32),
        ],
        scratch_shapes=[pltpu.VMEM((CLASS_ROWS, 128), f32)],
        compiler_params=_cparams(("arbitrary", "arbitrary")),
        name="out_router",
    )(a, w_bf, x, gt, gain, sh, sc, wrt, rbias, cnt_in, tri, *tail_args)


def _row_copy_scatter(src_ref, dst_ref, pos_ref, sem, i, k):
    row = i * SUBLANES + k
    return pltpu.make_async_copy(src_ref.at[i, pl.ds(k, 1), :], dst_ref.at[pl.ds(pos_ref[0, 0, row], 1), :], sem)


def _for_each_row(tm, fn):
    def tile(i, carry):
        for k in range(SUBLANES):
            fn(i, k)
        return carry

    lax.fori_loop(0, tm // SUBLANES, tile, 0, unroll=DMA_UNROLL // SUBLANES)


def _dispatch_body(pos_ref, rows_ref, dst_in_ref, dst_ref, sem, *, tm):
    del dst_in_ref
    _for_each_row(tm, lambda i, k: _row_copy_scatter(rows_ref, dst_ref, pos_ref, sem, i, k).start(priority=k % 2))
    _for_each_row(tm, lambda i, k: _row_copy_scatter(rows_ref, dst_ref, pos_ref, sem, i, k).wait())


def _dispatch(rows, pos3, sorted_rows, *, tm):
    n = rows.shape[0] * SUBLANES
    body = functools.partial(_dispatch_body, tm=tm)
    return pl.pallas_call(
        body,
        grid=(n // tm,),
        in_specs=[
            pl.BlockSpec((1, 1, tm), lambda i: (i, 0, 0), memory_space=pltpu.SMEM),
            pl.BlockSpec((tm // SUBLANES, SUBLANES, ROW_WORDS), lambda i: (i, 0, 0)),
            pl.BlockSpec(memory_space=pl.ANY),
        ],
        out_specs=pl.BlockSpec(memory_space=pl.ANY),
        out_shape=jax.ShapeDtypeStruct(sorted_rows.shape, sorted_rows.dtype),
        scratch_shapes=[pltpu.SemaphoreType.DMA(())],
        input_output_aliases={2: 0},
        compiler_params=_cparams(("arbitrary",)),
        name="moe_dispatch",
    )(pos3, rows, sorted_rows)


def _moe_body(ea_ref, eb_ref, valid_ref, xs_ref, w1a_ref, w3a_ref, w2a_ref, w1b_ref, w3b_ref, w2b_ref,
              ys_ref):
    del ea_ref, eb_ref
    i = pl.program_id(0)

    @pl.when(valid_ref[i] > 0)
    def _():
        left, right = _unpack_rows(xs_ref[:, :HALF])
        x = jnp.concatenate([left.astype(bf16), right.astype(bf16)], axis=1)
        gates = pltpu.bitcast(xs_ref[:, HALF:], f32)

        def expert(w1_ref, w3_ref, w2_ref, gate):
            h1 = _dot(x, w1_ref[0])
            h3 = _dot(x, w3_ref[0])
            hid = (h1 * _sigmoid(h1)) * h3 * gate
            return _dot(hid.astype(bf16), w2_ref[0])

        y = expert(w1a_ref, w3a_ref, w2a_ref, gates[:, 0:1]) + expert(w1b_ref, w3b_ref, w2b_ref, gates[:, 1:2])
        ys_ref[...] = _pack_rows(y)

    @pl.when(valid_ref[i] == 0)
    def _():
        ys_ref[...] = jnp.zeros(ys_ref.shape, u32)


def _moe(sorted_rows, tile_ea, tile_eb, tile_valid, w1_bf, w3_bf, w2_bf, *, tm):
    p = sorted_rows.shape[0]
    d, f = D_MODEL, D_EXPERT

    def wa(i, ea, eb, valid):
        return (ea[i], 0, 0)

    def wb(i, ea, eb, valid):
        return (eb[i], 0, 0)

    grid_spec = pltpu.PrefetchScalarGridSpec(
        num_scalar_prefetch=3,
        grid=(p // tm,),
        in_specs=[
            pl.BlockSpec((tm, ROW_WORDS), lambda i, ea, eb, valid: (i, 0)),
            pl.BlockSpec((1, d, f), wa), pl.BlockSpec((1, d, f), wa), pl.BlockSpec((1, f, d), wa),
            pl.BlockSpec((1, d, f), wb), pl.BlockSpec((1, d, f), wb), pl.BlockSpec((1, f, d), wb),
        ],
        out_specs=pl.BlockSpec((tm, HALF), lambda i, ea, eb, valid: (i, 0)),
    )
    return pl.pallas_call(
        _moe_body,
        grid_spec=grid_spec,
        out_shape=jax.ShapeDtypeStruct((p, HALF), u32),
        compiler_params=_cparams(("arbitrary",)),
        name="moe_experts",
    )(tile_ea, tile_eb, tile_valid, sorted_rows, w1_bf, w3_bf, w2_bf, w1_bf, w3_bf, w2_bf)


def _row_copy_gather(src_ref, dst_ref, pos_ref, sem, i, k):
    row = i * SUBLANES + k
    return pltpu.make_async_copy(src_ref.at[pl.ds(pos_ref[0, 0, row], 1), :], dst_ref.at[i, pl.ds(k, 1), :], sem)


def _combine_body(pos_ref, pos_next_ref, ys_ref, x_ref, gt_ref, fg_ref, o_ref, buf, sems, *, bs, tt, final_norm):
    tm = bs * tt
    nt = pl.num_programs(1)
    step = pl.program_id(0) * nt + pl.program_id(1)
    nsteps = pl.num_programs(0) * nt
    slot = step % 2

    def issue(p_ref, s):
        _for_each_row(tm, lambda i, k: _row_copy_gather(ys_ref, buf.at[s], p_ref, sems.at[s], i, k)
                      .start(priority=k % 2))

    @pl.when(step == 0)
    def _():
        issue(pos_ref, 0)

    @pl.when(step + 1 < nsteps)
    def _():
        issue(pos_next_ref, 1 - slot)

    _for_each_row(tm, lambda i, k: _row_copy_gather(ys_ref, buf.at[slot], pos_ref, sems.at[slot], i, k).wait())

    left, right = _unpack_rows(buf[slot].reshape(tm, HALF))
    y = jnp.concatenate([left, right], axis=1).reshape(bs, tt, D_MODEL)
    xn = _gated_add(x_ref[...], gt_ref[...], y)
    if final_norm:
        x2 = xn.reshape(tm, D_MODEL)
        xn = (x2 * _inv_rms(x2) * fg_ref[...]).reshape(bs, tt, D_MODEL)
    o_ref[...] = xn


def _combine(ys, pos3, x, gt, final_g, *, bs, tt, final_norm):
    bsz, seq, d = x.shape
    tm = bs * tt
    nt = seq // tt
    nsteps = (bsz // bs) * nt
    body = functools.partial(_combine_body, bs=bs, tt=tt, final_norm=final_norm)
    return pl.pallas_call(
        body,
        grid=(bsz // bs, nt),
        in_specs=[
            pl.BlockSpec((1, 1, tm), lambda b, t: (b * nt + t, 0, 0), memory_space=pltpu.SMEM),
            pl.BlockSpec((1, 1, tm), lambda b, t: (jnp.minimum(b * nt + t + 1, nsteps - 1), 0, 0),
                         memory_space=pltpu.SMEM),
            pl.BlockSpec(memory_space=pl.ANY),
            pl.BlockSpec((bs, tt, d), lambda b, t: (b, t, 0)),
            pl.BlockSpec((bs, SUBLANES, d), lambda b, t: (b, 0, 0)),
            pl.BlockSpec((1, d), lambda b, t: (0, 0)),
        ],
        out_specs=pl.BlockSpec((bs, tt, d), lambda b, t: (b, t, 0)),
        out_shape=jax.ShapeDtypeStruct((bsz, seq, d), f32),
        scratch_shapes=[pltpu.VMEM((2, tm // SUBLANES, SUBLANES, HALF), u32), pltpu.SemaphoreType.DMA((2,))],
        compiler_params=_cparams(("arbitrary", "arbitrary")),
        name="moe_combine",
    )(pos3, pos3, ys, x, gt, final_g)


def _rwkv_proj_body(x_ref, sh_ref, sc_ref, g_ref, mu_ref, shift_ref,
                    wr_ref, wk_ref, wv_ref, w1_ref, a1_ref, g1_ref, w2_ref, a2_ref, g2_ref,
                    vec_ref, ones_ref,
                    r_o, k_o, v_o, kk_o, b_o, g_o, lw_o, sh_o,
                    hs_scr, mix_scr, l1w_scr, l1a_scr, l1g_scr, *, bs, tt, tn):
    tm = bs * tt
    t = pl.program_id(1)
    j = pl.program_id(2)
    prev_row = CARRY_ROWS - 1

    @pl.when(j == 0)
    def _():
        @pl.when(t == 0)
        def _():
            hs_scr[:, prev_row:CARRY_ROWS, :] = shift_ref[...]

        h = _norm_mod(x_ref[...], g_ref[...], sh_ref[...], sc_ref[...])
        hs_scr[:, CARRY_ROWS:, :] = h
        h_prev = hs_scr[:, prev_row:prev_row + tt, :]
        h4 = h.reshape(bs, tt // SUBLANES, SUBLANES, D_MODEL)
        xx4 = h_prev.reshape(h4.shape) - h4
        for m in range(N_SHIFT_MIX):
            mix_scr[m] = (h4 + xx4 * mu_ref[m]).reshape(tm, D_MODEL).astype(bf16)
        last = hs_scr[:, prev_row + tt:CARRY_ROWS + tt, :]
        hs_scr[:, prev_row:CARRY_ROWS, :] = last
        sh_o[...] = last
        l1w_scr[...] = jnp.tanh(_dot(mix_scr[1], w1_ref[...])).astype(bf16)
        l1a_scr[...] = _dot(mix_scr[4], a1_ref[...]).astype(bf16)
        l1g_scr[...] = _sigmoid(_dot(mix_scr[5], g1_ref[...])).astype(bf16)

    vec = vec_ref[...]
    w0, a0, k_k, k_a = (vec[i * SUBLANES:(i + 1) * SUBLANES] for i in range(4))
    tiles = lambda x: x.reshape(tm // SUBLANES, SUBLANES, tn)
    r = _dot(mix_scr[0], wr_ref[...])
    k = tiles(_dot(mix_scr[2], wk_ref[...]))
    v = _dot(mix_scr[3], wv_ref[...])
    wl = w0 + tiles(_dot(l1w_scr[...], w2_ref[...]))
    a = _sigmoid(a0 + tiles(_dot(l1a_scr[...], a2_ref[...])))
    g = _dot(l1g_scr[...], g2_ref[...])
    neg = -wl
    softplus = jnp.maximum(neg, 0.0) + jnp.log(1.0 + jnp.exp(-jnp.abs(neg)))
    w_log = -softplus - 0.5
    lw = -jnp.exp(w_log)
    kkr = k * k_k
    ss = tiles(_dot((kkr * kkr).reshape(tm, tn).astype(bf16), ones_ref[...]))
    kk = kkr * lax.rsqrt(jnp.maximum(ss, 1e-24))
    k2 = k * (1.0 + (a - 1.0) * k_a)
    shp = (bs, tt, tn)
    r_o[...] = r.reshape(shp).astype(bf16)
    k_o[...] = k2.reshape(shp).astype(bf16)
    v_o[...] = v.reshape(shp).astype(bf16)
    kk_o[...] = kk.reshape(shp).astype(bf16)
    b_o[...] = (kk * a).reshape(shp).astype(bf16)
    g_o[...] = g.reshape(shp).astype(bf16)
    lw_o[...] = lw.reshape(shp)


def _rwkv_proj(x, sh, sc, gain, mu, shift, wr, wk, wv, w1p, a1p, g1, w2p, a2p, g2, vec, ones_bd,
               *, bs, tt, tn=256):
    bsz, seq, d = x.shape
    tm = bs * tt
    dg = g1.shape[1]
    body = functools.partial(_rwkv_proj_body, bs=bs, tt=tt, tn=tn)
    const2 = lambda b, t, j: (0, 0)
    colblk = lambda b, t, j: (0, j)
    tok = lambda b, t, j: (b, t, j)
    act = lambda dt: jax.ShapeDtypeStruct((bsz, seq, d), dt)
    return pl.pallas_call(
        body,
        grid=(bsz // bs, seq // tt, d // tn),
        in_specs=[
            pl.BlockSpec((bs, tt, d), lambda b, t, j: (b, t, 0)),
            pl.BlockSpec((bs, SUBLANES, d), lambda b, t, j: (b, 0, 0)),
            pl.BlockSpec((bs, SUBLANES, d), lambda b, t, j: (b, 0, 0)),
            pl.BlockSpec((1, d), const2),
            pl.BlockSpec((N_SHIFT_MIX, SUBLANES, d), lambda b, t, j: (0, 0, 0)),
            pl.BlockSpec((bs, 1, d), lambda b, t, j: (b, 0, 0)),
            pl.BlockSpec((d, tn), colblk), pl.BlockSpec((d, tn), colblk), pl.BlockSpec((d, tn), colblk),
            pl.BlockSpec((d, LORA_PAD), const2), pl.BlockSpec((d, LORA_PAD), const2),
            pl.BlockSpec((d, dg), const2),
            pl.BlockSpec((LORA_PAD, tn), colblk), pl.BlockSpec((LORA_PAD, tn), colblk),
            pl.BlockSpec((dg, tn), colblk),
            pl.BlockSpec((4 * SUBLANES, tn), colblk),
            pl.BlockSpec((tn, tn), const2),
        ],
        out_specs=[pl.BlockSpec((bs, tt, tn), tok)] * 7 + [pl.BlockSpec((bs, 1, d), lambda b, t, j: (b, 0, 0))],
        out_shape=[act(bf16)] * 6 + [act(f32), jax.ShapeDtypeStruct((bsz, 1, d), f32)],
        scratch_shapes=[
            pltpu.VMEM((bs, tt + CARRY_ROWS, d), f32),
            pltpu.VMEM((N_SHIFT_MIX, tm, d), bf16),
            pltpu.VMEM((tm, LORA_PAD), bf16),
            pltpu.VMEM((tm, LORA_PAD), bf16),
            pltpu.VMEM((tm, dg), bf16),
        ],
        compiler_params=_cparams(("arbitrary", "arbitrary", "arbitrary")),
        name="rwkv_proj",
    )(x, sh, sc, gain, mu, shift, wr, wk, wv, w1p, a1p, g1, w2p, a2p, g2, vec, ones_bd)


def _block_diag_rows(x, heads):
    return jnp.concatenate([jnp.where(m, x, 0.0) for m in heads], axis=1).astype(bf16)


def _bmm(a, b):
    return lax.dot_general(a, b, (((2,), (1,)), ((0,), (0,))), preferred_element_type=f32)


def _bmm_nt(a, b):
    return lax.dot_general(a, b, (((2,), (2,)), ((0,), (0,))), preferred_element_type=f32)


def _wkv_chunk(r, k, v, kk, beta, lw, c, c_end, state, masks):
    heads, strict, incl, eye, bd_mask = masks
    n = WKV_CHUNK
    bd = lambda x: _block_diag_rows(x, heads)
    g_inv = jnp.exp(-c)
    g_end = jnp.exp(c_end - c)
    a_t = -(kk * jnp.exp(c - lw))
    r_t = r * jnp.exp(c)
    ar = jnp.concatenate([a_t, r_t], axis=1).astype(bf16)
    a12 = _bmm_nt(ar, jnp.concatenate([bd(beta * g_inv), bd(k * g_inv)], axis=1))
    a_ab = jnp.where(strict, a12[:, :n, :PAIR], 0.0)
    a_rb = jnp.where(incl, a12[:, n:, :PAIR], 0.0)
    a_ak = jnp.where(strict, a12[:, :n, PAIR:], 0.0)
    a_rk = jnp.where(incl, a12[:, n:, PAIR:], 0.0)
    from_v = _bmm(jnp.concatenate([a_ak, a_rk], axis=1).astype(bf16), bd(v))
    av, ov = from_v[:, :n], from_v[:, n:]
    q = a_ab
    p = eye + q
    q = _bmm(q.astype(bf16), bd(q))
    for level in range(5):
        if level < 4:
            res = _bmm(jnp.concatenate([p, q], axis=1).astype(bf16), bd(q))
            p = p + res[:, :n]
            q = res[:, n:]
        else:
            p = p + _bmm(p.astype(bf16), bd(q))
    p_bf = p.astype(bf16)
    both = _bmm(p_bf, jnp.concatenate([bd(a_t), bd(av)], axis=2))
    a_bar, u0 = both[:, :, :PAIR], both[:, :, PAIR:]
    s_bf = state.astype(bf16)
    from_state = _bmm_nt(jnp.concatenate([a_bar, r_t], axis=1).astype(bf16), s_bf)
    u = from_state[:, :n] + u0
    o = from_state[:, n:] + ov + _bmm(a_rb.astype(bf16), bd(u))
    uv_t = jnp.swapaxes(jnp.concatenate([u, v], axis=1), 1, 2).astype(bf16)
    bk = jnp.concatenate([beta * g_end, k * g_end], axis=1).astype(bf16)
    decay = jnp.exp(c_end)
    new_state = (state * jnp.concatenate([decay] * WKV_HEADS, axis=1)
                 + jnp.where(bd_mask, _bmm(uv_t, bk), 0.0))
    return o, new_state


def _wkv_body(r_ref, k_ref, v_ref, kk_ref, b_ref, g_ref, lw_ref, s0_ref, vec_ref, tril_ref, ones_ref,
              z_ref, so_ref, s_scr, *, tc):
    t = pl.program_id(1)
    n = WKV_CHUNK

    @pl.when(t == 0)
    def _():
        s_scr[...] = s0_ref[0]

    lane = lax.broadcasted_iota(i32, (n, PAIR), 1)
    row = lax.broadcasted_iota(i32, (n, PAIR), 0)
    head_of = lambda idx: jnp.right_shift(idx, HEAD_SIZE.bit_length() - 1)
    heads = [head_of(lane) == h for h in range(WKV_HEADS)]
    within = lane & (HEAD_SIZE - 1)
    strict = within < row
    incl = within <= row
    eye = (within == row).astype(f32)
    bd_mask = (head_of(lax.broadcasted_iota(i32, (PAIR, PAIR), 0))
               == head_of(lax.broadcasted_iota(i32, (PAIR, PAIR), 1)))
    masks = (heads, strict, incl, eye, bd_mask)
    tril = tril_ref[...]
    ones_bd = ones_ref[...]
    rows = min(tc, n)
    nchunks = max(tc // n, 1)

    def units(x):
        return jnp.stack([x[:, p * PAIR:(p + 1) * PAIR] for p in range(N_PAIRS)], axis=0)

    def seg_sum(x):
        wide = jnp.concatenate([jnp.concatenate([x[2 * i], x[2 * i + 1]], axis=1)
                                for i in range(N_PAIRS // 2)], axis=0)
        s = _dot(wide.astype(bf16), ones_bd)
        return jnp.stack([s[(p // 2) * n:(p // 2 + 1) * n, (p % 2) * PAIR:(p % 2 + 1) * PAIR]
                          for p in range(N_PAIRS)], axis=0)

    vec = units(vec_ref[...])
    ln_w, ln_b, r_k = (vec[:, i * SUBLANES:(i + 1) * SUBLANES] for i in range(3))

    def load(ref, ci):
        x = ref[0, ci * n:ci * n + rows, :].astype(f32)
        if rows < n:
            x = jnp.concatenate([x, jnp.zeros((n - rows, x.shape[1]), f32)], axis=0)
        return x

    for ci in range(nchunks):
        lw = load(lw_ref, ci)
        hi, lo = _split2(lw)
        sums = _dot(tril, hi) + _dot(tril, lo)
        total = jnp.broadcast_to(sums[n - 1:n, :], sums.shape)
        r, k, v = (units(load(ref, ci)) for ref in (r_ref, k_ref, v_ref))
        o, s_new = _wkv_chunk(r, k, v, units(load(kk_ref, ci)), units(load(b_ref, ci)), units(lw),
                              units(sums), units(total), s_scr[...], masks)
        s_scr[...] = s_new
        mean = seg_sum(o) * (1.0 / HEAD_SIZE)
        dev = o - mean
        var = seg_sum(dev * dev) * (1.0 / HEAD_SIZE)
        y = _per_seq(dev * lax.rsqrt(var + GN_EPS), lambda y4, w4, b4: y4 * w4 + b4, ln_w, ln_b)
        y = y + seg_sum(_per_seq(r * k, lambda x4, rk4: x4 * rk4, r_k)) * v
        z = (y * units(load(g_ref, ci))).astype(bf16)
        z_ref[0, ci * n:ci * n + rows, :] = jnp.concatenate([z[p, :rows] for p in range(N_PAIRS)], axis=1)

    @pl.when(t == pl.num_programs(1) - 1)
    def _():
        so_ref[0] = s_scr[...]


def _wkv(r, k, v, kk, beta, g, lw, state_bd, vec, tril, ones_bd, *, tc):
    bsz, seq, d = r.shape
    body = functools.partial(_wkv_body, tc=tc)
    tok = pl.BlockSpec((1, tc, d), lambda b, t: (b, t, 0))
    st = pl.BlockSpec((1, N_PAIRS, PAIR, PAIR), lambda b, t: (b, 0, 0, 0))
    return pl.pallas_call(
        body,
        grid=(bsz, seq // tc),
        in_specs=[tok] * 7 + [
            st,
            pl.BlockSpec((3 * SUBLANES, d), lambda b, t: (0, 0)),
            pl.BlockSpec((WKV_CHUNK, WKV_CHUNK), lambda b, t: (0, 0)),
            pl.BlockSpec((2 * PAIR, 2 * PAIR), lambda b, t: (0, 0)),
        ],
        out_specs=[tok, st],
        out_shape=[jax.ShapeDtypeStruct((bsz, seq, d), bf16),
                   jax.ShapeDtypeStruct(state_bd.shape, f32)],
        scratch_shapes=[pltpu.VMEM((N_PAIRS, PAIR, PAIR), f32)],
        compiler_params=_cparams(("arbitrary", "arbitrary")),
        name="wkv",
    )(r, k, v, kk, beta, g, lw, state_bd, vec, tril, ones_bd)


def _block_ones(n):
    idx = jnp.arange(n) // HEAD_SIZE
    return (idx[:, None] == idx[None, :]).astype(bf16)


def _state_to_pairs(s):
    b = s.shape[0]
    s5 = s.reshape(b, N_PAIRS, WKV_HEADS, HEAD_SIZE, HEAD_SIZE)
    zero = jnp.zeros((b, N_PAIRS, HEAD_SIZE, HEAD_SIZE), s.dtype)
    rows = [jnp.concatenate([s5[:, :, h] if j == h else zero for j in range(WKV_HEADS)], axis=-1)
            for h in range(WKV_HEADS)]
    return jnp.concatenate(rows, axis=-2)


def _pairs_to_state(bd):
    b = bd.shape[0]
    blocks = [bd[:, :, h * HEAD_SIZE:(h + 1) * HEAD_SIZE, h * HEAD_SIZE:(h + 1) * HEAD_SIZE]
              for h in range(WKV_HEADS)]
    return jnp.stack(blocks, axis=2).reshape(b, N_HEADS, HEAD_SIZE, HEAD_SIZE)


_PAIR_TABLE = ((0, 1), (0, 2), (0, 3), (1, 2), (1, 3), (2, 3))


def _moe_plan(counts, n_tiles, tm):
    cnt = counts[:N_CLASSES].astype(i32)
    tiles_per = (cnt + tm - 1) // tm
    tile_end = jnp.cumsum(tiles_per)
    tile_start = tile_end - tiles_per
    offsets = tile_start * tm
    tidx = jnp.arange(n_tiles, dtype=i32)
    cls_of_tile = jnp.sum((tidx[:, None] >= tile_end[None, :]).astype(i32), axis=1)
    valid = (cls_of_tile < N_CLASSES).astype(i32)
    last_cls = jnp.max(jnp.where(tiles_per > 0, jnp.arange(N_CLASSES, dtype=i32), 0))
    cls_c = jnp.where(valid > 0, cls_of_tile, last_cls)
    grp = cls_c // 6
    pr = cls_c % 6
    lo_tab = jnp.array([p[0] for p in _PAIR_TABLE], i32)
    hi_tab = jnp.array([p[1] for p in _PAIR_TABLE], i32)
    ea = grp * EXPERTS_PER_GROUP + lo_tab[pr]
    eb = grp * EXPERTS_PER_GROUP + hi_tab[pr]
    return offsets, ea, eb, valid


def _positions(info, offsets):
    cls = info[:, 0, :].reshape(-1)
    rank = info[:, 1, :].reshape(-1)
    return offsets[cls] + rank


def _router_weights(wg, bg, we, be):
    w = jnp.concatenate([wg, we], axis=1).T
    w = jnp.pad(w, ((0, ROUTER_ROWS - w.shape[0]), (0, 0)))
    hi, lo = _split2_outside(w)
    bias = jnp.pad(jnp.concatenate([bg, be]), (0, ROUTER_ROWS - N_GROUPS - N_EXPERTS))
    return jnp.concatenate([hi, lo], axis=0), bias.reshape(ROUTER_ROWS, 1)


def _tiles(bsz, seq, rows):
    tt = min(seq, rows)
    bs = max(rows // tt, 1)
    assert bsz % bs == 0 and seq % tt == 0
    return bs, tt


def kernel(x_prompt, x_sample, c_prompt, c_sample, state_conv, state_shift, state_wkv, ada_w, ada_b, norm_g, final_g, cv_in, cv_w, cv_out, rw_mu, rw_r, rw_k, rw_v, rw_o, rw_w0, rw_w1, rw_w2, rw_a0, rw_a1, rw_a2, rw_g1, rw_g2, rw_kk, rw_ka, rw_rk, rw_lnw, rw_lnb, moe_wg, moe_bg, moe_we, moe_be, moe_w1, moe_w3, moe_w2):
    d = D_MODEL
    xs = [x_prompt, x_sample]
    bszs = [x.shape[0] for x in xs]
    seqs = [x.shape[1] for x in xs]
    ntok = [b * s for b, s in zip(bszs, seqs)]
    n_total = sum(ntok)

    c_all = jnp.concatenate([c_prompt, c_sample], axis=0)
    rows_c = c_all.shape[0]
    rows_pad = -(-rows_c // 8) * 8
    mod = _ada(jnp.pad(c_all, ((0, rows_pad - rows_c), (0, 0))), ada_w, ada_b)

    def mods(layer, trunk):
        lo = 0 if trunk == 0 else bszs[0]
        m = mod[layer, lo:lo + bszs[trunk]]
        return [jnp.broadcast_to(m[:, None, i * d:(i + 1) * d], (bszs[trunk], SUBLANES, d))
                for i in range(N_ADA)]

    conv_states = [jnp.zeros((1, bszs[0], CONV_WIDTH - 1, d), f32), state_conv]
    shift_states = [jnp.zeros((1, bszs[0], d), f32), state_shift]
    wkv_states = [jnp.zeros((1, bszs[0], N_HEADS, HEAD_SIZE, HEAD_SIZE), f32), state_wkv]

    tri_cache = {}

    def tri(n):
        if n not in tri_cache:
            tri_cache[n] = (jnp.arange(n)[:, None] < jnp.arange(n)[None, :]).astype(bf16)
        return tri_cache[n]

    n_tiles = n_total // MOE_TILE + N_CLASSES
    p_rows = n_tiles * MOE_TILE
    expert_w = [w.reshape((-1,) + w.shape[2:]).astype(bf16) for w in (moe_w1, moe_w3, moe_w2)]

    def moe_layer(layer, rows_list, info_list, counts):
        offsets, ea, eb, valid = _moe_plan(counts[:, 0], n_tiles, MOE_TILE)
        pos_list = [_positions(info, offsets) for info in info_list]
        sorted_rows = jnp.zeros((p_rows, ROW_WORDS), u32)
        for rows, pos in zip(rows_list, pos_list):
            sorted_rows = _dispatch(rows.reshape(-1, SUBLANES, ROW_WORDS), pos.reshape(-1, 1, MOE_TILE),
                                    sorted_rows, tm=MOE_TILE)
        first = layer * N_EXPERTS
        ys = _moe(sorted_rows, ea + first, eb + first, valid, *expert_w, tm=MOE_TILE)
        return ys, pos_list

    def out_router_pair(layer, acts, w_bf, xcur, tails=(None, None)):
        wrt, rbias = _router_weights(moe_wg[layer], moe_bg[layer], moe_we[layer], moe_be[layer])
        counts = jnp.zeros((CLASS_ROWS, 128), f32)
        x_new, rows_list, info_list = [], [], []
        for trunk in range(2):
            _, _, gt_m, sh_f, sc_f, _ = mods(layer, trunk)
            bs, tt = _tiles(bszs[trunk], seqs[trunk], ROUTER_TILE)
            xo, rows, info, counts = _out_router(
                acts[trunk], w_bf, xcur[trunk], gt_m, norm_g[layer, 1].reshape(1, d), sh_f, sc_f,
                wrt, rbias, counts, tri(bs * tt // ROUTER_SPLIT), bs=bs, tt=tt, tail=tails[trunk])
            x_new.append(xo)
            rows_list.append(rows)
            info_list.append(info)
        return x_new, rows_list, info_list, counts

    def combine_pair(layer, ys, pos_list, xcur, final):
        out = []
        for trunk in range(2):
            gt_f = mods(layer, trunk)[5]
            bs, tt = _tiles(bszs[trunk], seqs[trunk], COMBINE_TILE)
            out.append(_combine(ys, pos_list[trunk].reshape(-1, 1, bs * tt), xcur[trunk], gt_f,
                                final_g.reshape(1, d), bs=bs, tt=tt, final_norm=final))
        return out

    w_in_hi, w_in_lo = _split2_outside(cv_in[0])
    w_out_hi, w_out_lo = _split2_outside(cv_out[0])
    gain0 = norm_g[0, 0].reshape(1, d)
    conv_taps = jnp.repeat(cv_w[0], SUBLANES, axis=0)
    z0, conv_out = [], []
    for trunk in range(2):
        sh_m, sc_m = mods(0, trunk)[:2]
        bs, tt = _tiles(bszs[trunk], seqs[trunk], CONV_TILE)
        st8 = jnp.pad(conv_states[trunk][0], ((0, 0), (CARRY_ROWS - (CONV_WIDTH - 1), 0), (0, 0)))
        z, so = _conv_in(xs[trunk], sh_m, sc_m, gain0, cv_in[0].astype(bf16), conv_taps, st8, bs=bs, tt=tt)
        z0.append(z)
        conv_out.append(so[:, -1, CARRY_ROWS - (CONV_WIDTH - 1):, :][None])
    assert min(seqs) >= TAIL_WINDOW
    x_tail = jnp.concatenate([x[:, -TAIL_WINDOW:, :] for x in xs], axis=0)
    n_seq = x_tail.shape[0]
    mod_tail = [jnp.concatenate([mods(0, trunk)[i] for trunk in range(2)], axis=0) for i in range(3)]
    z_tail, _ = _conv_in(x_tail, mod_tail[0], mod_tail[1], gain0, w_in_hi, conv_taps,
                         jnp.zeros((n_seq, CARRY_ROWS, d), f32), bs=n_seq, tt=TAIL_WINDOW, w_in_lo=w_in_lo)
    x1_tail = _tail_out(z_tail, x_tail, mod_tail[2], w_out_hi, w_out_lo)[:, TAIL_WINDOW - TAIL_ROWS:, :]
    tails = (x1_tail[:bszs[0]], x1_tail[bszs[0]:])
    x1, rows0, info0, counts0 = out_router_pair(0, z0, cv_out[0].astype(bf16), xs, tails)
    ys0, pos0 = moe_layer(0, rows0, info0, counts0)
    x2 = combine_pair(0, ys0, pos0, x1, False)

    pad_l = ((0, 0), (0, LORA_PAD - rw_w1.shape[2]))
    pad_r = ((0, LORA_PAD - rw_w2.shape[1]), (0, 0))
    w1p = jnp.pad(rw_w1[0], pad_l).astype(bf16)
    a1p = jnp.pad(rw_a1[0], pad_l).astype(bf16)
    w2p = jnp.pad(rw_w2[0], pad_r).astype(bf16)
    a2p = jnp.pad(rw_a2[0], pad_r).astype(bf16)
    vec_p = jnp.repeat(jnp.stack([rw_w0[0], rw_a0[0], rw_kk[0], rw_ka[0]]), SUBLANES, axis=0)
    vec_w = jnp.repeat(jnp.stack([rw_lnw[0], rw_lnb[0], rw_rk[0].reshape(d)]), SUBLANES, axis=0)
    wr_bf, wk_bf, wv_bf = rw_r[0].astype(bf16), rw_k[0].astype(bf16), rw_v[0].astype(bf16)
    g1_bf, g2_bf = rw_g1[0].astype(bf16), rw_g2[0].astype(bf16)
    proj_tn = PROJ_COLS
    tril = (jnp.arange(WKV_CHUNK)[:, None] >= jnp.arange(WKV_CHUNK)[None, :]).astype(bf16)
    z1, shift_out, wkv_out = [], [], []
    for trunk in range(2):
        sh_m, sc_m = mods(1, trunk)[:2]
        bs, tt = _tiles(bszs[trunk], seqs[trunk], PROJ_TILE)
        r, k, v, kk, beta, g, lw, sho = _rwkv_proj(
            x2[trunk], sh_m, sc_m, norm_g[1, 0].reshape(1, d),
            jnp.broadcast_to(rw_mu[0][:, None, :], (N_SHIFT_MIX, SUBLANES, d)),
            shift_states[trunk][0].reshape(bszs[trunk], 1, d), wr_bf, wk_bf, wv_bf, w1p, a1p, g1_bf,
            w2p, a2p, g2_bf, vec_p, _block_ones(proj_tn), bs=bs, tt=tt, tn=proj_tn)
        tc = min(seqs[trunk], WKV_CHUNKS_PER_STEP * WKV_CHUNK)
        z, s_bd = _wkv(r, k, v, kk, beta, g, lw, _state_to_pairs(wkv_states[trunk][0]), vec_w, tril,
                       _block_ones(2 * PAIR), tc=tc)
        z1.append(z)
        shift_out.append(sho.reshape(1, bszs[trunk], d))
        wkv_out.append(_pairs_to_state(s_bd)[None])
    x3, rows1, info1, counts1 = out_router_pair(1, z1, rw_o[0].astype(bf16), x2)
    ys1, pos1 = moe_layer(1, rows1, info1, counts1)
    y = combine_pair(1, ys1, pos1, x3, True)

    return (y[0], y[1], conv_out[0], shift_out[0], wkv_out[0], conv_out[1], shift_out[1], wkv_out[1])
```
